```python
import math
import jax, jax.numpy as jnp
from jax import lax
import numpy as np

D_MODEL = 1024
BATCH = 16
SEQ = 2048
DEPTH = 1
DEC_BATCH = 128
DEC_SEQ = 8
PAST_LEN = 16384
PAGE_SIZE = 128

D_MIX = D_MODEL
D_ATT = D_MIX // 2
D_RET = D_MIX - D_ATT
HEAD_DIM_A = 64
N_HEADS_A = D_ATT // HEAD_DIM_A
N_KV_HEADS_A = 2
GROUP_A = N_HEADS_A // N_KV_HEADS_A
KV_W = N_KV_HEADS_A * HEAD_DIM_A
WINDOW = 128
BLOCK_A = WINDOW
N_HEADS_R = 4
HEAD_DIM_R = D_RET // N_HEADS_R
CHUNK_R = 128
ROPE_BASE = 10000.0
LN_EPS = 1e-5
GN_EPS = 1e-5
DEEPNORM_ALPHA = float((2 * DEPTH) ** 0.25)
DEEPNORM_BETA = float((8 * DEPTH) ** -0.25)
SPLIT_SIZES = (D_ATT, KV_W, KV_W, D_ATT, D_RET, D_RET, D_RET, D_RET)
D_IN = sum(SPLIT_SIZES)
SPLIT_POINTS = [int(s) for s in np.cumsum(SPLIT_SIZES)[:-1]]

kernel_name = "hymba_swa_sink_retention_deepnorm_adaln_step"


def _front(x, c, w_ada, b_ada, w_in):
    B, T, _ = x.shape
    cond = jax.nn.silu(c) @ w_ada + b_ada
    shift, scale, gate = jnp.split(cond, 3, axis=-1)
    h = x * (1.0 + scale[:, None, :]) + shift[:, None, :]
    z = h @ w_in
    q_a, k_a, v_a, g_a, q_r, k_r, v_r, g_r = jnp.split(z, SPLIT_POINTS, axis=-1)
    q_a = q_a.reshape(B, T, N_KV_HEADS_A, GROUP_A, HEAD_DIM_A)
    k_a = k_a.reshape(B, T, N_KV_HEADS_A, HEAD_DIM_A)
    v_a = v_a.reshape(B, T, N_KV_HEADS_A, HEAD_DIM_A)
    q_r = q_r.reshape(B, T, N_HEADS_R, HEAD_DIM_R)
    k_r = k_r.reshape(B, T, N_HEADS_R, HEAD_DIM_R)
    v_r = v_r.reshape(B, T, N_HEADS_R, HEAD_DIM_R)
    return (q_a, k_a, v_a, g_a, q_r, k_r, v_r, g_r), gate


def _back(x, gate, o_a, g_a, o_r, g_r, gn_w, w_out, ln_w, ln_b):
    B, T, _ = x.shape
    o_r = o_r.astype(jnp.float32)
    mu = o_r.mean(-1, keepdims=True)
    var = jnp.square(o_r - mu).mean(-1, keepdims=True)
    o_r = ((o_r - mu) * lax.rsqrt(var + GN_EPS)).reshape(B, T, D_RET).astype(x.dtype) * gn_w
    o_a = o_a.reshape(B, T, D_ATT).astype(x.dtype)
    mixed = jnp.concatenate([o_a * jax.nn.silu(g_a), o_r * jax.nn.silu(g_r)], axis=-1)
    y = mixed @ w_out
    r = (DEEPNORM_ALPHA * x + gate[:, None, :] * y).astype(jnp.float32)
    mu = r.mean(-1, keepdims=True)
    var = jnp.square(r - mu).mean(-1, keepdims=True)
    rn = (r - mu) * lax.rsqrt(var + LN_EPS)
    return (rn * ln_w + ln_b).astype(x.dtype)


def _rotary(x, pos):
    half = x.shape[-1] // 2
    inv = ROPE_BASE ** (-jnp.arange(half, dtype=jnp.float32) / half)
    ang = pos[:, None] * inv[None, :]
    cos = jnp.cos(ang)[:, None, :]
    sin = jnp.sin(ang)[:, None, :]
    xf = x.astype(jnp.float32)
    x1, x2 = xf[..., :half], xf[..., half:]
    return jnp.concatenate([x1 * cos - x2 * sin, x2 * cos + x1 * sin], axis=-1).astype(x.dtype)


def _sink_attention(q, k, v, qpos, kpos, sinks):
    s = jnp.einsum('bnqhgd,bnkhd->bnhgqk', q, k).astype(jnp.float32) * (HEAD_DIM_A ** -0.5)
    rel = qpos[:, :, None] - kpos[:, None, :]
    valid = (rel >= 0) & (rel <= WINDOW) & (kpos[:, None, :] >= 0)
    s = jnp.where(valid[None, :, None, None], s, -1e30)
    sink = sinks.astype(jnp.float32).reshape(1, 1, N_KV_HEADS_A, GROUP_A, 1, 1)
    m = jnp.maximum(s.max(-1, keepdims=True), sink)
    p = jnp.exp(s - m)
    p = p / (p.sum(-1, keepdims=True) + jnp.exp(sink - m))
    return jnp.einsum('bnhgqk,bnkhd->bnqhgd', p.astype(v.dtype), v)


def _swa_prompt(q, k, v, sinks):
    B, T = q.shape[0], q.shape[1]
    nb = T // BLOCK_A
    qb = q.reshape(B, nb, BLOCK_A, N_KV_HEADS_A, GROUP_A, HEAD_DIM_A)
    kb = k.reshape(B, nb, BLOCK_A, N_KV_HEADS_A, HEAD_DIM_A)
    vb = v.reshape(B, nb, BLOCK_A, N_KV_HEADS_A, HEAD_DIM_A)
    kk = jnp.concatenate([jnp.concatenate([jnp.zeros_like(kb[:, :1]), kb[:, :-1]], axis=1), kb], axis=2)
    vv = jnp.concatenate([jnp.concatenate([jnp.zeros_like(vb[:, :1]), vb[:, :-1]], axis=1), vb], axis=2)
    start = jnp.arange(nb)[:, None] * BLOCK_A
    qpos = start + jnp.arange(BLOCK_A)[None, :]
    kpos = start - BLOCK_A + jnp.arange(2 * BLOCK_A)[None, :]
    o = _sink_attention(qb, kk, vv, qpos, kpos, sinks)
    return o.reshape(B, T, N_KV_HEADS_A, GROUP_A, HEAD_DIM_A)


def _swa_sample(q, k, v, cache_k, cache_v, sinks):
    keys = jnp.concatenate([cache_k.astype(k.dtype), k], axis=1)
    vals = jnp.concatenate([cache_v.astype(v.dtype), v], axis=1)
    T = q.shape[1]
    qpos = (PAST_LEN + jnp.arange(T))[None, :]
    kpos = (PAST_LEN - WINDOW + jnp.arange(WINDOW + T))[None, :]
    o = _sink_attention(q[:, None], keys[:, None], vals[:, None], qpos, kpos, sinks)[:, 0]
    return o, keys[:, -WINDOW:], vals[:, -WINDOW:]


def _retention_chunk(q, k, v, S, log_gamma):
    L = q.shape[1]
    idx = jnp.arange(L, dtype=jnp.float32)
    diff = idx[:, None] - idx[None, :]
    dmat = jnp.where(diff >= 0, jnp.exp(log_gamma[:, None, None] * jnp.maximum(diff, 0.0)), 0.0)
    qf, kf, vf, Sf = (a.astype(jnp.float32) for a in (q, k, v, S))
    s = jnp.einsum('bihd,bjhd->bhij', qf, kf) * dmat[None]
    intra = jnp.einsum('bhij,bjhe->bihe', s, vf)
    q_decay = jnp.exp(log_gamma[None, :] * (idx[:, None] + 1.0))
    inter = jnp.einsum('bihd,bhde->bihe', qf, Sf) * q_decay[None, :, :, None]
    k_decay = jnp.exp(log_gamma[None, :] * (L - 1.0 - idx[:, None]))
    S_new = jnp.exp(log_gamma * L)[None, :, None, None] * Sf + jnp.einsum(
        'bjhd,bjhe->bhde', kf * k_decay[None, :, :, None], vf)
    return intra + inter, S_new


def _retention_prompt(q, k, v, log_gamma):
    B, T = q.shape[0], q.shape[1]
    nc = T // CHUNK_R

    def to_chunks(a):
        return jnp.moveaxis(a.reshape(B, nc, CHUNK_R, N_HEADS_R, HEAD_DIM_R), 1, 0)

    def step(S, qkv):
        qc, kc, vc = qkv
        o, S = _retention_chunk(qc, kc, vc, S, log_gamma)
        return S, o

    S0 = jnp.zeros((B, N_HEADS_R, HEAD_DIM_R, HEAD_DIM_R), jnp.float32)
    S, o = lax.scan(step, S0, (to_chunks(q), to_chunks(k), to_chunks(v)))
    o = jnp.moveaxis(o, 0, 1).reshape(B, T, N_HEADS_R, HEAD_DIM_R)
    return o, S


def setup_inputs(seed: int = 0) -> dict:
    key = jax.random.key(seed)
    ks = jax.random.split(key, 16)
    nrm = jax.random.normal
    f32 = jnp.float32
    return {
        "x_prompt": nrm(ks[0], (BATCH, SEQ, D_MODEL), f32),
        "x_sample": nrm(ks[1], (DEC_BATCH, DEC_SEQ, D_MODEL), f32),
        "c_prompt": nrm(ks[2], (BATCH, D_MODEL), f32),
        "c_sample": nrm(ks[3], (DEC_BATCH, D_MODEL), f32),
        "cache_k_win": nrm(ks[4], (DEPTH, DEC_BATCH, WINDOW, N_KV_HEADS_A, HEAD_DIM_A), f32),
        "cache_v_win": nrm(ks[5], (DEPTH, DEC_BATCH, WINDOW, N_KV_HEADS_A, HEAD_DIM_A), f32),
        "state_ret": 0.5 * nrm(ks[6], (DEPTH, DEC_BATCH, N_HEADS_R, HEAD_DIM_R, HEAD_DIM_R), f32),
        "w_ada": 0.5 * D_MODEL ** -0.5 * nrm(ks[7], (DEPTH, D_MODEL, 3 * D_MODEL), f32),
        "b_ada": 0.02 * nrm(ks[8], (DEPTH, 3 * D_MODEL), f32),
        "w_in": D_MODEL ** -0.5 * nrm(ks[9], (DEPTH, D_MODEL, D_IN), f32),
        "attn_sinks": 0.5 * nrm(ks[10], (DEPTH, N_HEADS_A), f32),
        "ret_gn_w": 1.0 + 0.02 * nrm(ks[11], (DEPTH, D_RET), f32),
        "w_out": DEEPNORM_BETA * D_MIX ** -0.5 * nrm(ks[12], (DEPTH, D_MIX, D_MODEL), f32),
        "ln_w": 1.0 + 0.02 * nrm(ks[13], (DEPTH, D_MODEL), f32),
        "ln_b": 0.02 * nrm(ks[14], (DEPTH, D_MODEL), f32),
    }


def reference(x_prompt, x_sample, c_prompt, c_sample, cache_k_win, cache_v_win, state_ret,
              w_ada, b_ada, w_in, attn_sinks, ret_gn_w, w_out, ln_w, ln_b):
    log_gamma = jnp.log(1.0 - 2.0 ** (-5.0 - jnp.arange(N_HEADS_R, dtype=jnp.float32)))
    pos_p = jnp.arange(x_prompt.shape[1], dtype=jnp.float32)
    pos_s = (PAST_LEN + jnp.arange(x_sample.shape[1])).astype(jnp.float32)
    x_p, x_s = x_prompt, x_sample
    kp_l, vp_l, sp_l, ks_l, vs_l, ss_l = [], [], [], [], [], []
    for l in range(DEPTH):
        (q_a, k_a, v_a, g_a, q_r, k_r, v_r, g_r), gate = _front(x_p, c_prompt, w_ada[l], b_ada[l], w_in[l])
        o_a = _swa_prompt(q_a, k_a, v_a, attn_sinks[l])
        q_r = _rotary(q_r, pos_p)
        k_r = _rotary(k_r, pos_p) * (HEAD_DIM_R ** -0.5)
        o_r, S_p = _retention_prompt(q_r, k_r, v_r, log_gamma)
        kp_l.append(k_a[:, -WINDOW:])
        vp_l.append(v_a[:, -WINDOW:])
        sp_l.append(S_p)
        x_p = _back(x_p, gate, o_a, g_a, o_r, g_r, ret_gn_w[l], w_out[l], ln_w[l], ln_b[l])
        (q_a, k_a, v_a, g_a, q_r, k_r, v_r, g_r), gate = _front(x_s, c_sample, w_ada[l], b_ada[l], w_in[l])
        o_a, k_buf, v_buf = _swa_sample(q_a, k_a, v_a, cache_k_win[l], cache_v_win[l], attn_sinks[l])
        q_r = _rotary(q_r, pos_s)
        k_r = _rotary(k_r, pos_s) * (HEAD_DIM_R ** -0.5)
        o_r, S_s = _retention_chunk(q_r, k_r, v_r, state_ret[l], log_gamma)
        ks_l.append(k_buf)
        vs_l.append(v_buf)
        ss_l.append(S_s)
        x_s = _back(x_s, gate, o_a, g_a, o_r, g_r, ret_gn_w[l], w_out[l], ln_w[l], ln_b[l])
    return (x_p, x_s, jnp.stack(kp_l), jnp.stack(vp_l), jnp.stack(sp_l),
            jnp.stack(ks_l), jnp.stack(vs_l), jnp.stack(ss_l))
```

```python
import functools

import jax
import jax.numpy as jnp
import numpy as np
from jax import lax
from jax.experimental import pallas as pl
from jax.experimental.pallas import tpu as pltpu

D_MODEL = 1024
D_ATT = 512
D_RET = 512
HEAD_DIM_A = 64
N_HEADS_A = 8
N_KV_HEADS_A = 2
GROUP_A = 4
KV_W = 128
WINDOW = 128
N_HEADS_R = 4
HEAD_DIM_R = 128
CHUNK_R = 128
ROPE_BASE = 10000.0
LN_EPS = 1e-5
GN_EPS = 1e-5
PAST_LEN = 16384
MASKED = -1e30

OFF_QA = 0
OFF_KA = OFF_QA + D_ATT
OFF_VA = OFF_KA + KV_W
OFF_GA = OFF_VA + KV_W
OFF_QR = OFF_GA + D_ATT
OFF_KR = OFF_QR + D_RET
OFF_VR = OFF_KR + D_RET
OFF_GR = OFF_VR + D_RET
D_IN = OFF_GR + D_RET

PROMPT_BLOCK = 256
SAMPLE_BATCH_BLOCK = 8
COND_COL_BLOCK = 512
VMEM_LIMIT_BYTES = 48 * 1024 * 1024

F32 = jnp.float32
BF16 = jnp.bfloat16


def _silu(x):
    return x * jax.nn.sigmoid(x)


def _dot(a, b):
    return jnp.dot(a, b, preferred_element_type=F32)


def _dot_nt(a, b):
    return lax.dot_general(a, b, (((1,), (1,)), ((), ())), preferred_element_type=F32)


def _dot_tn(a, b):
    return lax.dot_general(a, b, (((0,), (0,)), ((), ())), preferred_element_type=F32)


def _rotate(x, cos_full, sin_signed):
    return x * cos_full + pltpu.roll(x, HEAD_DIM_R // 2, 1) * sin_signed


def _group_norm(o):
    mu = jnp.mean(o, axis=-1, keepdims=True)
    d = o - mu
    var = jnp.mean(d * d, axis=-1, keepdims=True)
    return d * lax.rsqrt(var + GN_EPS)


def _deepnorm_out(x, gate, y, alpha, ln_w, ln_b):
    r = alpha * x + gate * y
    mu = jnp.mean(r, axis=-1, keepdims=True)
    d = r - mu
    var = jnp.mean(d * d, axis=-1, keepdims=True)
    return d * lax.rsqrt(var + LN_EPS) * ln_w + ln_b


def _sink_softmax(s, sink):
    m = jnp.maximum(jnp.max(s, axis=-1, keepdims=True), sink)
    p = jnp.exp(s - m)
    denom = jnp.sum(p, axis=-1, keepdims=True) + jnp.exp(sink - m)
    return p.astype(BF16), denom


def _cond_kernel(c_ref, w_ref, b_ref, o_ref):
    a = _silu(c_ref[...]).astype(BF16)
    o_ref[...] = _dot(a, w_ref[...].astype(BF16)) + b_ref[...]


def _cond_call(c_all, w_ada, b_ada):
    rows = c_all.shape[0]
    n_out = w_ada.shape[1]
    return pl.pallas_call(
        _cond_kernel,
        out_shape=jax.ShapeDtypeStruct((rows, n_out), F32),
        grid=(n_out // COND_COL_BLOCK,),
        in_specs=[
            pl.BlockSpec((rows, D_MODEL), lambda j: (0, 0)),
            pl.BlockSpec((D_MODEL, COND_COL_BLOCK), lambda j: (0, j)),
            pl.BlockSpec((1, COND_COL_BLOCK), lambda j: (0, j)),
        ],
        out_specs=pl.BlockSpec((rows, COND_COL_BLOCK), lambda j: (0, j)),
        compiler_params=pltpu.CompilerParams(dimension_semantics=("arbitrary",)),
        name="adaln_cond",
    )(c_all, w_ada, b_ada.reshape(1, n_out))


def _prompt_kernel(sinks_ref, gl_ref, x_ref, cond_ref, w_in_ref, w_out_ref, gnw_ref, lnw_ref, lnb_ref,
                   cos_ref, sin_ref, dmat_ref, qdec_ref, kdec_ref,
                   y_ref, kwin_ref, vwin_ref, state_ref, kprev_ref, vprev_ref, *, alpha):
    t = pl.program_id(1)
    n_sub = PROMPT_BLOCK // WINDOW

    @pl.when(t == 0)
    def _():
        kprev_ref[...] = jnp.zeros_like(kprev_ref)
        vprev_ref[...] = jnp.zeros_like(vprev_ref)
        state_ref[...] = jnp.zeros_like(state_ref)

    x = x_ref[0]
    shift = cond_ref[0, :, 0:D_MODEL]
    scale = cond_ref[0, :, D_MODEL:2 * D_MODEL]
    gate = cond_ref[0, :, 2 * D_MODEL:3 * D_MODEL]
    h = (x * (1.0 + scale) + shift).astype(BF16)

    def proj(off, width):
        return _dot(h, w_in_ref[:, off:off + width])

    zk = proj(OFF_KA, KV_W)
    zv = proj(OFF_VA, KV_W)

    @pl.when(t == pl.num_programs(1) - 1)
    def _():
        kwin_ref[0] = zk[PROMPT_BLOCK - WINDOW:]
        vwin_ref[0] = zv[PROMPT_BLOCK - WINDOW:]

    zk_b = zk.astype(BF16)
    zv_b = zv.astype(BF16)
    zq = (proj(OFF_QA, D_ATT) * (HEAD_DIM_A ** -0.5)).astype(BF16)

    qi = lax.broadcasted_iota(jnp.int32, (WINDOW, 2 * WINDOW), 0)
    kj = lax.broadcasted_iota(jnp.int32, (WINDOW, 2 * WINDOW), 1)
    band = (kj >= qi) & (kj <= qi + WINDOW)

    o_a_blocks = []
    for s in range(n_sub):
        r0 = s * WINDOW
        k_prev = kprev_ref[...] if s == 0 else zk_b[r0 - WINDOW:r0]
        v_prev = vprev_ref[...] if s == 0 else zv_b[r0 - WINDOW:r0]
        kk = jnp.concatenate([k_prev, zk_b[r0:r0 + WINDOW]], axis=0)
        vv = jnp.concatenate([v_prev, zv_b[r0:r0 + WINDOW]], axis=0)
        start = t * PROMPT_BLOCK + r0
        valid = band & (kj >= WINDOW - start)
        heads = []
        for hk in range(N_KV_HEADS_A):
            kh = kk[:, hk * HEAD_DIM_A:(hk + 1) * HEAD_DIM_A]
            vh = vv[:, hk * HEAD_DIM_A:(hk + 1) * HEAD_DIM_A]
            qs = jnp.concatenate(
                [zq[r0:r0 + WINDOW, (hk * GROUP_A + g) * HEAD_DIM_A:(hk * GROUP_A + g + 1) * HEAD_DIM_A]
                 for g in range(GROUP_A)], axis=0)
            sc = _dot_nt(qs, kh)
            ps, denoms = [], []
            for g in range(GROUP_A):
                sg = jnp.where(valid, sc[g * WINDOW:(g + 1) * WINDOW], MASKED)
                p, denom = _sink_softmax(sg, sinks_ref[hk * GROUP_A + g])
                ps.append(p)
                denoms.append(denom)
            o = _dot(jnp.concatenate(ps, axis=0), vh)
            for g in range(GROUP_A):
                heads.append(o[g * WINDOW:(g + 1) * WINDOW] / denoms[g])
        o_a_blocks.append(jnp.concatenate(heads, axis=-1))
    kprev_ref[...] = zk_b[PROMPT_BLOCK - WINDOW:]
    vprev_ref[...] = zv_b[PROMPT_BLOCK - WINDOW:]
    o_a = jnp.concatenate(o_a_blocks, axis=0)
    mixed_a = (o_a * _silu(proj(OFF_GA, D_ATT))).astype(BF16)

    zqr = proj(OFF_QR, D_RET)
    zkr = proj(OFF_KR, D_RET)
    zvr = proj(OFF_VR, D_RET)
    cos_full = cos_ref[...]
    sin_signed = sin_ref[...]
    head_cols = []
    for hh in range(N_HEADS_R):
        cs = slice(hh * HEAD_DIM_R, (hh + 1) * HEAD_DIM_R)
        q_rot = _rotate(zqr[:, cs], cos_full, sin_signed)
        k_rot = _rotate(zkr[:, cs], cos_full, sin_signed) * (HEAD_DIM_R ** -0.5)
        v_b = zvr[:, cs].astype(BF16)
        chunks = []
        for s in range(n_sub):
            rs = slice(s * CHUNK_R, (s + 1) * CHUNK_R)
            q_b = q_rot[rs].astype(BF16)
            k_c = k_rot[rs]
            v_c = v_b[rs]
            state = state_ref[0, hh]
            sc = (_dot_nt(q_b, k_c.astype(BF16)) * dmat_ref[hh]).astype(BF16)
            o = _dot(sc, v_c) + _dot(q_b, state.astype(BF16)) * qdec_ref[hh]
            state_ref[0, hh] = gl_ref[hh] * state + _dot_tn((k_c * kdec_ref[hh]).astype(BF16), v_c)
            chunks.append(o)
        o_h = _group_norm(jnp.concatenate(chunks, axis=0))
        head_cols.append(o_h)
    o_r = jnp.concatenate(head_cols, axis=-1) * gnw_ref[...]
    mixed_r = (o_r * _silu(proj(OFF_GR, D_RET))).astype(BF16)

    y = _dot(mixed_a, w_out_ref[0:D_ATT, :]) + _dot(mixed_r, w_out_ref[D_ATT:, :])
    y_ref[0] = _deepnorm_out(x, gate, y, alpha, lnw_ref[...], lnb_ref[...])


def _prompt_call(x, cond, sinks, w_in_b, w_out_b, gn_w, ln_w, ln_b, consts, alpha):
    batch, seq, _ = x.shape
    n_t = seq // PROMPT_BLOCK
    smem = pl.BlockSpec(memory_space=pltpu.SMEM)
    whole = lambda shape: pl.BlockSpec(shape, lambda b, t: (0,) * len(shape))
    return pl.pallas_call(
        functools.partial(_prompt_kernel, alpha=alpha),
        out_shape=(
            jax.ShapeDtypeStruct((batch, seq, D_MODEL), F32),
            jax.ShapeDtypeStruct((batch, WINDOW, KV_W), F32),
            jax.ShapeDtypeStruct((batch, WINDOW, KV_W), F32),
            jax.ShapeDtypeStruct((batch, N_HEADS_R, HEAD_DIM_R, HEAD_DIM_R), F32),
        ),
        grid=(batch, n_t),
        in_specs=[
            smem, smem,
            pl.BlockSpec((1, PROMPT_BLOCK, D_MODEL), lambda b, t: (b, t, 0)),
            pl.BlockSpec((1, 1, 3 * D_MODEL), lambda b, t: (b, 0, 0)),
            whole((D_MODEL, D_IN)),
            whole((D_MODEL, D_MODEL)),
            whole((1, D_RET)),
            whole((1, D_MODEL)),
            whole((1, D_MODEL)),
            pl.BlockSpec((PROMPT_BLOCK, HEAD_DIM_R), lambda b, t: (t, 0)),
            pl.BlockSpec((PROMPT_BLOCK, HEAD_DIM_R), lambda b, t: (t, 0)),
            whole((N_HEADS_R, CHUNK_R, CHUNK_R)),
            whole((N_HEADS_R, CHUNK_R, HEAD_DIM_R)),
            whole((N_HEADS_R, CHUNK_R, HEAD_DIM_R)),
        ],
        out_specs=(
            pl.BlockSpec((1, PROMPT_BLOCK, D_MODEL), lambda b, t: (b, t, 0)),
            pl.BlockSpec((1, WINDOW, KV_W), lambda b, t: (b, 0, 0)),
            pl.BlockSpec((1, WINDOW, KV_W), lambda b, t: (b, 0, 0)),
            pl.BlockSpec((1, N_HEADS_R, HEAD_DIM_R, HEAD_DIM_R), lambda b, t: (b, 0, 0, 0)),
        ),
        scratch_shapes=[pltpu.VMEM((WINDOW, KV_W), BF16), pltpu.VMEM((WINDOW, KV_W), BF16)],
        compiler_params=pltpu.CompilerParams(
            dimension_semantics=("arbitrary", "arbitrary"), vmem_limit_bytes=VMEM_LIMIT_BYTES),
        name="prompt_layer",
    )(sinks, consts["gl_chunk"], x, cond, w_in_b, w_out_b, gn_w, ln_w, ln_b,
      consts["cos_p"], consts["sin_p"], consts["dmat"], consts["qdec"], consts["kdec"])


def _sample_kernel(sinks_ref, gl_ref, x_ref, cond_ref, ck_ref, cv_ref, sin_state_ref, w_in_ref, w_out_ref,
                   gnw_ref, lnw_ref, lnb_ref, cos_ref, sin_ref, mask_ref, dmat_ref, qdec_ref, kdec_ref,
                   y_ref, kwin_ref, vwin_ref, state_ref, *, alpha, dec_seq):
    bb = SAMPLE_BATCH_BLOCK
    rows = bb * dec_seq
    x3 = x_ref[...]
    shift = cond_ref[:, :, 0:D_MODEL]
    scale = cond_ref[:, :, D_MODEL:2 * D_MODEL]
    gate3 = cond_ref[:, :, 2 * D_MODEL:3 * D_MODEL]
    h = (x3 * (1.0 + scale) + shift).reshape(rows, D_MODEL).astype(BF16)
    x = x3.reshape(rows, D_MODEL)
    gate = jnp.broadcast_to(gate3, (bb, dec_seq, D_MODEL)).reshape(rows, D_MODEL)

    def proj(off, width):
        return _dot(h, w_in_ref[:, off:off + width])

    zk = proj(OFF_KA, KV_W)
    zv = proj(OFF_VA, KV_W)
    ck = ck_ref[...]
    cv = cv_ref[...]
    kwin_ref[:, 0:WINDOW - dec_seq, :] = ck[:, dec_seq:, :]
    vwin_ref[:, 0:WINDOW - dec_seq, :] = cv[:, dec_seq:, :]
    kwin_ref[:, WINDOW - dec_seq:, :] = zk.reshape(bb, dec_seq, KV_W)
    vwin_ref[:, WINDOW - dec_seq:, :] = zv.reshape(bb, dec_seq, KV_W)
    keys = jnp.concatenate([ck.reshape(bb * WINDOW, KV_W).astype(BF16), zk.astype(BF16)], axis=0)
    vals = jnp.concatenate([cv.reshape(bb * WINDOW, KV_W).astype(BF16), zv.astype(BF16)], axis=0)
    zq = (proj(OFF_QA, D_ATT) * (HEAD_DIM_A ** -0.5)).astype(BF16)
    valid = mask_ref[...] > 0.5
    row_idx = lax.broadcasted_iota(jnp.int32, (GROUP_A * rows, 1), 0)
    heads = []
    for hk in range(N_KV_HEADS_A):
        kh = keys[:, hk * HEAD_DIM_A:(hk + 1) * HEAD_DIM_A]
        vh = vals[:, hk * HEAD_DIM_A:(hk + 1) * HEAD_DIM_A]
        qs = jnp.concatenate(
            [zq[:, (hk * GROUP_A + g) * HEAD_DIM_A:(hk * GROUP_A + g + 1) * HEAD_DIM_A] for g in range(GROUP_A)],
            axis=0)
        sink = jnp.full((GROUP_A * rows, 1), sinks_ref[hk * GROUP_A], F32)
        for g in range(1, GROUP_A):
            sink = jnp.where(row_idx >= g * rows, sinks_ref[hk * GROUP_A + g], sink)
        sc = jnp.where(valid, _dot_nt(qs, kh), MASKED)
        p, denom = _sink_softmax(sc, sink)
        o = _dot(p, vh) / denom
        for g in range(GROUP_A):
            heads.append(o[g * rows:(g + 1) * rows])
    o_a = jnp.concatenate(heads, axis=-1)
    mixed_a = (o_a * _silu(proj(OFF_GA, D_ATT))).astype(BF16)

    zqr = proj(OFF_QR, D_RET)
    zkr = proj(OFF_KR, D_RET)
    zvr = proj(OFF_VR, D_RET)
    cos_full = cos_ref[...]
    sin_signed = sin_ref[...]
    head_cols = []
    for hh in range(N_HEADS_R):
        cs = slice(hh * HEAD_DIM_R, (hh + 1) * HEAD_DIM_R)
        q_rot = _rotate(zqr[:, cs], cos_full, sin_signed)
        k_rot = _rotate(zkr[:, cs], cos_full, sin_signed) * (HEAD_DIM_R ** -0.5)
        v_b = zvr[:, cs].astype(BF16)
        seqs = []
        for b in range(bb):
            rs = slice(b * dec_seq, (b + 1) * dec_seq)
            q_b = q_rot[rs].astype(BF16)
            k_c = k_rot[rs]
            v_c = v_b[rs]
            state = sin_state_ref[b, hh]
            sc = (_dot_nt(q_b, k_c.astype(BF16)) * dmat_ref[hh]).astype(BF16)
            o = _dot(sc, v_c) + _dot(q_b, state.astype(BF16)) * qdec_ref[hh]
            state_ref[b, hh] = gl_ref[hh] * state + _dot_tn((k_c * kdec_ref[hh]).astype(BF16), v_c)
            seqs.append(o)
        head_cols.append(_group_norm(jnp.concatenate(seqs, axis=0)))
    o_r = jnp.concatenate(head_cols, axis=-1) * gnw_ref[...]
    mixed_r = (o_r * _silu(proj(OFF_GR, D_RET))).astype(BF16)

    y = _dot(mixed_a, w_out_ref[0:D_ATT, :]) + _dot(mixed_r, w_out_ref[D_ATT:, :])
    out = _deepnorm_out(x, gate, y, alpha, lnw_ref[...], lnb_ref[...])
    y_ref[...] = out.reshape(bb, dec_seq, D_MODEL)


def _sample_call(x, cond, cache_k, cache_v, state, sinks, w_in_b, w_out_b, gn_w, ln_w, ln_b, consts, alpha):
    batch, dec_seq, _ = x.shape
    bb = SAMPLE_BATCH_BLOCK
    rows = bb * dec_seq
    n_keys = bb * WINDOW + rows
    smem = pl.BlockSpec(memory_space=pltpu.SMEM)
    whole = lambda shape: pl.BlockSpec(shape, lambda i: (0,) * len(shape))
    return pl.pallas_call(
        functools.partial(_sample_kernel, alpha=alpha, dec_seq=dec_seq),
        out_shape=(
            jax.ShapeDtypeStruct((batch, dec_seq, D_MODEL), F32),
            jax.ShapeDtypeStruct((batch, WINDOW, KV_W), F32),
            jax.ShapeDtypeStruct((batch, WINDOW, KV_W), F32),
            jax.ShapeDtypeStruct((batch, N_HEADS_R, HEAD_DIM_R, HEAD_DIM_R), F32),
        ),
        grid=(batch // bb,),
        in_specs=[
            smem, smem,
            pl.BlockSpec((bb, dec_seq, D_MODEL), lambda i: (i, 0, 0)),
            pl.BlockSpec((bb, 1, 3 * D_MODEL), lambda i: (i, 0, 0)),
            pl.BlockSpec((bb, WINDOW, KV_W), lambda i: (i, 0, 0)),
            pl.BlockSpec((bb, WINDOW, KV_W), lambda i: (i, 0, 0)),
            pl.BlockSpec((bb, N_HEADS_R, HEAD_DIM_R, HEAD_DIM_R), lambda i: (i, 0, 0, 0)),
            whole((D_MODEL, D_IN)),
            whole((D_MODEL, D_MODEL)),
            whole((1, D_RET)),
            whole((1, D_MODEL)),
            whole((1, D_MODEL)),
            whole((rows, HEAD_DIM_R)),
            whole((rows, HEAD_DIM_R)),
            whole((GROUP_A * rows, n_keys)),
            whole((N_HEADS_R, dec_seq, dec_seq)),
            whole((N_HEADS_R, dec_seq, HEAD_DIM_R)),
            whole((N_HEADS_R, dec_seq, HEAD_DIM_R)),
        ],
        out_specs=(
            pl.BlockSpec((bb, dec_seq, D_MODEL), lambda i: (i, 0, 0)),
            pl.BlockSpec((bb, WINDOW, KV_W), lambda i: (i, 0, 0)),
            pl.BlockSpec((bb, WINDOW, KV_W), lambda i: (i, 0, 0)),
            pl.BlockSpec((bb, N_HEADS_R, HEAD_DIM_R, HEAD_DIM_R), lambda i: (i, 0, 0, 0)),
        ),
        compiler_params=pltpu.CompilerParams(
            dimension_semantics=("arbitrary",), vmem_limit_bytes=VMEM_LIMIT_BYTES),
        name="sample_layer",
    )(sinks, consts["gl_dec"], x, cond, cache_k, cache_v, state, w_in_b, w_out_b, gn_w, ln_w, ln_b,
      consts["cos_s"], consts["sin_s"], consts["mask_s"], consts["dmat_s"], consts["qdec_s"], consts["kdec_s"])


def _rope_tables(pos):
    half = HEAD_DIM_R // 2
    inv = ROPE_BASE ** (-np.arange(half, dtype=np.float64) / half)
    ang = pos[:, None].astype(np.float64) * inv[None, :]
    cos, sin = np.cos(ang), np.sin(ang)
    return (np.concatenate([cos, cos], axis=1).astype(np.float32),
            np.concatenate([-sin, sin], axis=1).astype(np.float32))


def _decay_tables(length):
    log_gamma = np.log(1.0 - 2.0 ** (-5.0 - np.arange(N_HEADS_R, dtype=np.float64)))
    idx = np.arange(length, dtype=np.float64)
    diff = idx[:, None] - idx[None, :]
    dmat = np.where(diff >= 0, np.exp(log_gamma[:, None, None] * np.maximum(diff, 0.0)), 0.0)
    qdec = np.exp(log_gamma[:, None] * (idx[None, :] + 1.0))
    kdec = np.exp(log_gamma[:, None] * (length - 1.0 - idx[None, :]))
    lanes = lambda a: np.broadcast_to(a[:, :, None], (N_HEADS_R, length, HEAD_DIM_R)).astype(np.float32)
    return dmat.astype(np.float32), lanes(qdec), lanes(kdec), np.exp(log_gamma * length).astype(np.float32)


def _sample_mask(dec_seq):
    bb = SAMPLE_BATCH_BLOCK
    rows = bb * dec_seq
    r = np.arange(GROUP_A * rows)
    q_seq, q_tok = (r % rows) // dec_seq, r % dec_seq
    c = np.arange(bb * WINDOW)
    cached = (c[None, :] // WINDOW == q_seq[:, None]) & (c[None, :] % WINDOW >= q_tok[:, None])
    n = np.arange(rows)
    fresh = (n[None, :] // dec_seq == q_seq[:, None]) & (n[None, :] % dec_seq <= q_tok[:, None])
    return np.concatenate([cached, fresh], axis=1).astype(np.float32)


def _tables(seq, dec_seq):
    cos_p, sin_p = _rope_tables(np.arange(seq))
    cos_s, sin_s = _rope_tables(PAST_LEN + np.arange(dec_seq))
    dmat, qdec, kdec, gl_chunk = _decay_tables(CHUNK_R)
    dmat_s, qdec_s, kdec_s, gl_dec = _decay_tables(dec_seq)
    tile = lambda a: np.tile(a, (SAMPLE_BATCH_BLOCK, 1))
    return dict(cos_p=cos_p, sin_p=sin_p, cos_s=tile(cos_s), sin_s=tile(sin_s), mask_s=_sample_mask(dec_seq),
                dmat=dmat, qdec=qdec, kdec=kdec, gl_chunk=gl_chunk,
                dmat_s=dmat_s, qdec_s=qdec_s, kdec_s=kdec_s, gl_dec=gl_dec)


def kernel(x_prompt, x_sample, c_prompt, c_sample, cache_k_win, cache_v_win, state_ret,
           w_ada, b_ada, w_in, attn_sinks, ret_gn_w, w_out, ln_w, ln_b):
    depth = w_in.shape[0]
    batch, seq, _ = x_prompt.shape
    dec_batch, dec_seq, _ = x_sample.shape
    assert seq % PROMPT_BLOCK == 0 and dec_batch % SAMPLE_BATCH_BLOCK == 0 and dec_seq == 8
    alpha = float((2 * depth) ** 0.25)
    consts = {k: jnp.asarray(v) for k, v in _tables(seq, dec_seq).items()}
    c_all = jnp.concatenate([c_prompt, c_sample], axis=0)
    x_p, x_s = x_prompt, x_sample
    kp, vp, sp, ks, vs, ss = [], [], [], [], [], []
    for l in range(depth):
        cond = _cond_call(c_all, w_ada[l], b_ada[l])
        w_in_b = w_in[l].astype(BF16)
        w_out_b = w_out[l].astype(BF16)
        gn_w = ret_gn_w[l].reshape(1, D_RET)
        lw = ln_w[l].reshape(1, D_MODEL)
        lb = ln_b[l].reshape(1, D_MODEL)
        x_p, k_p, v_p, s_p = _prompt_call(
            x_p, cond[:batch].reshape(batch, 1, 3 * D_MODEL), attn_sinks[l], w_in_b, w_out_b, gn_w, lw, lb,
            consts, alpha)
        x_s, k_s, v_s, s_s = _sample_call(
            x_s, cond[batch:].reshape(dec_batch, 1, 3 * D_MODEL),
            cache_k_win[l].reshape(dec_batch, WINDOW, KV_W), cache_v_win[l].reshape(dec_batch, WINDOW, KV_W),
            state_ret[l], attn_sinks[l], w_in_b, w_out_b, gn_w, lw, lb, consts, alpha)
        kp.append(k_p.reshape(batch, WINDOW, N_KV_HEADS_A, HEAD_DIM_A))
        vp.append(v_p.reshape(batch, WINDOW, N_KV_HEADS_A, HEAD_DIM_A))
        sp.append(s_p)
        ks.append(k_s.reshape(dec_batch, WINDOW, N_KV_HEADS_A, HEAD_DIM_A))
        vs.append(v_s.reshape(dec_batch, WINDOW, N_KV_HEADS_A, HEAD_DIM_A))
        ss.append(s_s)
    return (x_p, x_s, jnp.stack(kp), jnp.stack(vp), jnp.stack(sp), jnp.stack(ks), jnp.stack(vs), jnp.stack(ss))
```

```python
import functools

import jax
import jax.numpy as jnp
import numpy as np
from jax import lax
from jax.experimental import pallas as pl
from jax.experimental.pallas import tpu as pltpu

D_MODEL = 1024
D_ATT = 512
D_RET = 512
HEAD_DIM_A = 64
N_HEADS_A = 8
N_KV_HEADS_A = 2
GROUP_A = 4
KV_W = 128
WINDOW = 128
N_HEADS_R = 4
HEAD_DIM_R = 128
CHUNK_R = 128
ROPE_BASE = 10000.0
LN_EPS = 1e-5
GN_EPS = 1e-5
PAST_LEN = 16384
MASKED = -1e30

OFF_QA = 0
OFF_KA = OFF_QA + D_ATT
OFF_VA = OFF_KA + KV_W
OFF_GA = OFF_VA + KV_W
OFF_QR = OFF_GA + D_ATT
OFF_KR = OFF_QR + D_RET
OFF_VR = OFF_KR + D_RET
OFF_GR = OFF_VR + D_RET
D_IN = OFF_GR + D_RET

PROMPT_BLOCK = 256
PROMPT_BATCH_BLOCK = 2
PROMPT_STAGE_SKEW = 9
DENSE_PIECE = 256
SAMPLE_BATCH_BLOCK = 8
COND_COL_BLOCK = 512
VMEM_LIMIT_BYTES = 48 * 1024 * 1024

F32 = jnp.float32
BF16 = jnp.bfloat16


def _silu(x):
    return x * jax.nn.sigmoid(x)


def _dot(a, b):
    return jnp.dot(a, b, preferred_element_type=F32)


def _dot_nt(a, b):
    return lax.dot_general(a, b, (((1,), (1,)), ((), ())), preferred_element_type=F32)


def _dot_tn(a, b):
    return lax.dot_general(a, b, (((0,), (0,)), ((), ())), preferred_element_type=F32)


def _rotate(x, cos_full, sin_signed):
    return x * cos_full + pltpu.roll(x, HEAD_DIM_R // 2, 1) * sin_signed


def _group_norm(o):
    mu = jnp.mean(o, axis=-1, keepdims=True)
    d = o - mu
    var = jnp.mean(d * d, axis=-1, keepdims=True)
    return d * lax.rsqrt(var + GN_EPS)


def _deepnorm_out(x, gate, y, alpha, ln_w, ln_b):
    r = alpha * x + gate * y
    mu = jnp.mean(r, axis=-1, keepdims=True)
    d = r - mu
    var = jnp.mean(d * d, axis=-1, keepdims=True)
    return d * lax.rsqrt(var + LN_EPS) * ln_w + ln_b


def _sink_softmax(s, sink):
    m = jnp.maximum(jnp.max(s, axis=-1, keepdims=True), sink)
    p = jnp.exp(s - m)
    denom = jnp.sum(p, axis=-1, keepdims=True) + jnp.exp(sink - m)
    return p.astype(BF16), denom


def _cond_kernel(c_ref, w_ref, b_ref, o_ref):
    a = _silu(c_ref[...]).astype(BF16)
    o_ref[...] = _dot(a, w_ref[...].astype(BF16)) + b_ref[...]


def _cond_call(c_all, w_ada, b_ada):
    rows = c_all.shape[0]
    n_out = w_ada.shape[1]
    return pl.pallas_call(
        _cond_kernel,
        out_shape=jax.ShapeDtypeStruct((rows, n_out), F32),
        grid=(n_out // COND_COL_BLOCK,),
        in_specs=[
            pl.BlockSpec((rows, D_MODEL), lambda j: (0, 0)),
            pl.BlockSpec((D_MODEL, COND_COL_BLOCK), lambda j: (0, j)),
            pl.BlockSpec((1, COND_COL_BLOCK), lambda j: (0, j)),
        ],
        out_specs=pl.BlockSpec((rows, COND_COL_BLOCK), lambda j: (0, j)),
        compiler_params=pltpu.CompilerParams(dimension_semantics=("arbitrary",)),
        name="adaln_cond",
    )(c_all, w_ada, b_ada.reshape(1, n_out))


def _prompt_kernel(sinks_ref, gl_ref, x_ref, cond_ref, w_in_ref, w_out_ref, gnw_ref, lnw_ref, lnb_ref,
                   cos_ref, sin_ref, dmat_ref, qdec_ref, kdec_ref,
                   y_ref, kwin_ref, vwin_ref, state_ref, kprev_ref, vprev_ref, *, alpha):
    t = pl.program_id(1)

    @pl.when(t == 0)
    def _():
        kprev_ref[...] = jnp.zeros_like(kprev_ref)
        vprev_ref[...] = jnp.zeros_like(vprev_ref)
        state_ref[...] = jnp.zeros_like(state_ref)

    stages = [_prompt_sequence_block(bi, t, sinks_ref, gl_ref, x_ref, cond_ref, w_in_ref, w_out_ref, gnw_ref,
                                     lnw_ref, lnb_ref, cos_ref, sin_ref, dmat_ref, qdec_ref, kdec_ref,
                                     y_ref, kwin_ref, vwin_ref, state_ref, kprev_ref, vprev_ref, alpha)
              for bi in range(PROMPT_BATCH_BLOCK)]
    live = list(range(PROMPT_BATCH_BLOCK))
    tick = 0
    while live:
        for bi in list(live):
            if tick >= bi * PROMPT_STAGE_SKEW and next(stages[bi], "done") == "done":
                live.remove(bi)
        tick += 1


def _prompt_sequence_block(bi, t, sinks_ref, gl_ref, x_ref, cond_ref, w_in_ref, w_out_ref, gnw_ref, lnw_ref, lnb_ref,
                           cos_ref, sin_ref, dmat_ref, qdec_ref, kdec_ref,
                           y_ref, kwin_ref, vwin_ref, state_ref, kprev_ref, vprev_ref, alpha):
    n_sub = PROMPT_BLOCK // WINDOW
    x = x_ref[bi]
    shift = cond_ref[bi, :, 0:D_MODEL]
    scale = cond_ref[bi, :, D_MODEL:2 * D_MODEL]
    gate = cond_ref[bi, :, 2 * D_MODEL:3 * D_MODEL]
    h = (x * (1.0 + scale) + shift).astype(BF16)

    def proj(off, width):
        return _dot(h, w_in_ref[:, off:off + width])

    pending = [(name, off + c)
               for name, off, width in (("ga", OFF_GA, D_ATT), ("qr", OFF_QR, D_RET), ("kr", OFF_KR, D_RET),
                                        ("vr", OFF_VR, D_RET), ("gr", OFF_GR, D_RET))
               for c in range(0, width, DENSE_PIECE)]
    n_pending = len(pending)
    pieces = {}

    def emit_dense(n):
        for _ in range(min(n, len(pending))):
            name, off = pending.pop(0)
            pieces.setdefault(name, []).append(proj(off, DENSE_PIECE))

    def section(name):
        return jnp.concatenate(pieces[name], axis=-1)

    zkv = proj(OFF_KA, 2 * KV_W)
    zk = zkv[:, 0:KV_W]
    zv = zkv[:, KV_W:2 * KV_W]

    @pl.when(t == pl.num_programs(1) - 1)
    def _():
        kwin_ref[bi] = zk[PROMPT_BLOCK - WINDOW:]
        vwin_ref[bi] = zv[PROMPT_BLOCK - WINDOW:]

    zk_b = zk.astype(BF16)
    zv_b = zv.astype(BF16)
    yield
    zq = (jnp.concatenate([proj(OFF_QA + c, DENSE_PIECE) for c in range(0, D_ATT, DENSE_PIECE)], axis=-1)
          * (HEAD_DIM_A ** -0.5)).astype(BF16)
    yield

    qi = lax.broadcasted_iota(jnp.int32, (WINDOW, 2 * WINDOW), 0)
    kj = lax.broadcasted_iota(jnp.int32, (WINDOW, 2 * WINDOW), 1)
    band = (kj >= qi) & (kj <= qi + WINDOW)

    units = []
    for s in range(n_sub):
        r0 = s * WINDOW
        k_prev = kprev_ref[bi] if s == 0 else zk_b[r0 - WINDOW:r0]
        v_prev = vprev_ref[bi] if s == 0 else zv_b[r0 - WINDOW:r0]
        kk = jnp.concatenate([k_prev, zk_b[r0:r0 + WINDOW]], axis=0)
        vv = jnp.concatenate([v_prev, zv_b[r0:r0 + WINDOW]], axis=0)
        start = t * PROMPT_BLOCK + r0
        valid = band & (kj >= WINDOW - start)
        for hk in range(N_KV_HEADS_A):
            kh = kk[:, hk * HEAD_DIM_A:(hk + 1) * HEAD_DIM_A]
            vh = vv[:, hk * HEAD_DIM_A:(hk + 1) * HEAD_DIM_A]
            qs = jnp.concatenate(
                [zq[r0:r0 + WINDOW, (hk * GROUP_A + g) * HEAD_DIM_A:(hk * GROUP_A + g + 1) * HEAD_DIM_A]
                 for g in range(GROUP_A)], axis=0)
            units.append((hk, valid, _dot_nt(qs, kh), vh))
    kprev_ref[bi] = zk_b[PROMPT_BLOCK - WINDOW:]
    vprev_ref[bi] = zv_b[PROMPT_BLOCK - WINDOW:]
    yield

    fill = -(-n_pending // len(units))
    softmaxed = []
    for hk, valid, sc, vh in units:
        ps, denoms = [], []
        for g in range(GROUP_A):
            sg = jnp.where(valid, sc[g * WINDOW:(g + 1) * WINDOW], MASKED)
            p, denom = _sink_softmax(sg, sinks_ref[hk * GROUP_A + g])
            ps.append(p)
            denoms.append(denom)
        softmaxed.append((jnp.concatenate(ps, axis=0), denoms, vh))
        emit_dense(fill)
        yield
    emit_dense(len(pending))

    heads = []
    for p, denoms, vh in softmaxed:
        o = _dot(p, vh)
        heads.extend(o[g * WINDOW:(g + 1) * WINDOW] / denoms[g] for g in range(GROUP_A))
    yield
    o_a = jnp.concatenate(
        [jnp.concatenate(heads[s * N_HEADS_A:(s + 1) * N_HEADS_A], axis=-1) for s in range(n_sub)], axis=0)
    mixed_a = (o_a * _silu(section("ga"))).astype(BF16)
    yield

    zqr, zkr, zvr = section("qr"), section("kr"), section("vr")
    cos_full = cos_ref[...]
    sin_signed = sin_ref[...]
    chunk_rows = [slice(s * CHUNK_R, (s + 1) * CHUNK_R) for s in range(n_sub)]
    q_b, v_b, raw, kv, y_a = [], [], [], [], []
    for hh in range(N_HEADS_R):
        cs = slice(hh * HEAD_DIM_R, (hh + 1) * HEAD_DIM_R)
        q_h = _rotate(zqr[:, cs], cos_full, sin_signed).astype(BF16)
        k_h = _rotate(zkr[:, cs], cos_full, sin_signed) * (HEAD_DIM_R ** -0.5)
        v_h = zvr[:, cs].astype(BF16)
        k_hb = k_h.astype(BF16)
        q_b.append(q_h)
        v_b.append(v_h)
        raw.append([_dot_nt(q_h[rs], k_hb[rs]) for rs in chunk_rows])
        kv.append([_dot_tn((k_h[rs] * kdec_ref[hh]).astype(BF16), v_h[rs]) for rs in chunk_rows])
        cols = slice(hh * DENSE_PIECE, (hh + 1) * DENSE_PIECE)
        y_a.append(_dot(mixed_a, w_out_ref[0:D_ATT, cols]))
        yield
    head_cols = []
    for hh in range(N_HEADS_R):
        state = state_ref[bi, hh]
        chunks = []
        for s, rs in enumerate(chunk_rows):
            sc = (raw[hh][s] * dmat_ref[hh]).astype(BF16)
            chunks.append(_dot(sc, v_b[hh][rs]) + _dot(q_b[hh][rs], state.astype(BF16)) * qdec_ref[hh])
            state = gl_ref[hh] * state + kv[hh][s]
        state_ref[bi, hh] = state
        head_cols.append(_group_norm(jnp.concatenate(chunks, axis=0)))
        yield
    o_r = jnp.concatenate(head_cols, axis=-1) * gnw_ref[...]
    mixed_r = (o_r * _silu(section("gr"))).astype(BF16)
    yield

    y = jnp.concatenate(y_a, axis=-1) + _dot(mixed_r, w_out_ref[D_ATT:, :])
    y_ref[bi] = _deepnorm_out(x, gate, y, alpha, lnw_ref[...], lnb_ref[...])


def _prompt_call(x, cond, sinks, w_in_b, w_out_b, gn_w, ln_w, ln_b, consts, alpha):
    batch, seq, _ = x.shape
    n_t = seq // PROMPT_BLOCK
    nb = PROMPT_BATCH_BLOCK
    smem = pl.BlockSpec(memory_space=pltpu.SMEM)
    whole = lambda shape: pl.BlockSpec(shape, lambda b, t: (0,) * len(shape))
    return pl.pallas_call(
        functools.partial(_prompt_kernel, alpha=alpha),
        out_shape=(
            jax.ShapeDtypeStruct((batch, seq, D_MODEL), F32),
            jax.ShapeDtypeStruct((batch, WINDOW, KV_W), F32),
            jax.ShapeDtypeStruct((batch, WINDOW, KV_W), F32),
            jax.ShapeDtypeStruct((batch, N_HEADS_R, HEAD_DIM_R, HEAD_DIM_R), F32),
        ),
        grid=(batch // nb, n_t),
        in_specs=[
            smem, smem,
            pl.BlockSpec((nb, PROMPT_BLOCK, D_MODEL), lambda b, t: (b, t, 0)),
            pl.BlockSpec((nb, 1, 3 * D_MODEL), lambda b, t: (b, 0, 0)),
            whole((D_MODEL, D_IN)),
            whole((D_MODEL, D_MODEL)),
            whole((1, D_RET)),
            whole((1, D_MODEL)),
            whole((1, D_MODEL)),
            pl.BlockSpec((PROMPT_BLOCK, HEAD_DIM_R), lambda b, t: (t, 0)),
            pl.BlockSpec((PROMPT_BLOCK, HEAD_DIM_R), lambda b, t: (t, 0)),
            whole((N_HEADS_R, CHUNK_R, CHUNK_R)),
            whole((N_HEADS_R, CHUNK_R, HEAD_DIM_R)),
            whole((N_HEADS_R, CHUNK_R, HEAD_DIM_R)),
        ],
        out_specs=(
            pl.BlockSpec((nb, PROMPT_BLOCK, D_MODEL), lambda b, t: (b, t, 0)),
            pl.BlockSpec((nb, WINDOW, KV_W), lambda b, t: (b, 0, 0)),
            pl.BlockSpec((nb, WINDOW, KV_W), lambda b, t: (b, 0, 0)),
            pl.BlockSpec((nb, N_HEADS_R, HEAD_DIM_R, HEAD_DIM_R), lambda b, t: (b, 0, 0, 0)),
        ),
        scratch_shapes=[pltpu.VMEM((nb, WINDOW, KV_W), BF16), pltpu.VMEM((nb, WINDOW, KV_W), BF16)],
        compiler_params=pltpu.CompilerParams(
            dimension_semantics=("arbitrary", "arbitrary"), vmem_limit_bytes=VMEM_LIMIT_BYTES),
        name="prompt_layer",
    )(sinks, consts["gl_chunk"], x, cond, w_in_b, w_out_b, gn_w, ln_w, ln_b,
      consts["cos_p"], consts["sin_p"], consts["dmat"], consts["qdec"], consts["kdec"])


def _sample_kernel(sinks_ref, gl_ref, x_ref, cond_ref, ck_ref, cv_ref, sin_state_ref, w_in_ref, w_out_ref,
                   gnw_ref, lnw_ref, lnb_ref, cos_ref, sin_ref, mask_ref, dmat_ref, qdec_ref, kdec_ref,
                   y_ref, kwin_ref, vwin_ref, state_ref, *, alpha, dec_seq):
    bb = SAMPLE_BATCH_BLOCK
    rows = bb * dec_seq
    x3 = x_ref[...]
    shift = cond_ref[:, :, 0:D_MODEL]
    scale = cond_ref[:, :, D_MODEL:2 * D_MODEL]
    gate3 = cond_ref[:, :, 2 * D_MODEL:3 * D_MODEL]
    h = (x3 * (1.0 + scale) + shift).reshape(rows, D_MODEL).astype(BF16)
    x = x3.reshape(rows, D_MODEL)
    gate = jnp.broadcast_to(gate3, (bb, dec_seq, D_MODEL)).reshape(rows, D_MODEL)

    def proj(off, width):
        return _dot(h, w_in_ref[:, off:off + width])

    zk = proj(OFF_KA, KV_W)
    zv = proj(OFF_VA, KV_W)
    ck = ck_ref[...]
    cv = cv_ref[...]
    kwin_ref[:, 0:WINDOW - dec_seq, :] = ck[:, dec_seq:, :]
    vwin_ref[:, 0:WINDOW - dec_seq, :] = cv[:, dec_seq:, :]
    kwin_ref[:, WINDOW - dec_seq:, :] = zk.reshape(bb, dec_seq, KV_W)
    vwin_ref[:, WINDOW - dec_seq:, :] = zv.reshape(bb, dec_seq, KV_W)
    keys = jnp.concatenate([ck.reshape(bb * WINDOW, KV_W).astype(BF16), zk.astype(BF16)], axis=0)
    vals = jnp.concatenate([cv.reshape(bb * WINDOW, KV_W).astype(BF16), zv.astype(BF16)], axis=0)
    zq = (proj(OFF_QA, D_ATT) * (HEAD_DIM_A ** -0.5)).astype(BF16)
    valid = mask_ref[...] > 0.5
    row_idx = lax.broadcasted_iota(jnp.int32, (GROUP_A * rows, 1), 0)
    heads = []
    for hk in range(N_KV_HEADS_A):
        kh = keys[:, hk * HEAD_DIM_A:(hk + 1) * HEAD_DIM_A]
        vh = vals[:, hk * HEAD_DIM_A:(hk + 1) * HEAD_DIM_A]
        qs = jnp.concatenate(
            [zq[:, (hk * GROUP_A + g) * HEAD_DIM_A:(hk * GROUP_A + g + 1) * HEAD_DIM_A] for g in range(GROUP_A)],
            axis=0)
        sink = jnp.full((GROUP_A * rows, 1), sinks_ref[hk * GROUP_A], F32)
        for g in range(1, GROUP_A):
            sink = jnp.where(row_idx >= g * rows, sinks_ref[hk * GROUP_A + g], sink)
        sc = jnp.where(valid, _dot_nt(qs, kh), MASKED)
        p, denom = _sink_softmax(sc, sink)
        o = _dot(p, vh) / denom
        for g in range(GROUP_A):
            heads.append(o[g * rows:(g + 1) * rows])
    o_a = jnp.concatenate(heads, axis=-1)
    mixed_a = (o_a * _silu(proj(OFF_GA, D_ATT))).astype(BF16)

    zqr = proj(OFF_QR, D_RET)
    zkr = proj(OFF_KR, D_RET)
    zvr = proj(OFF_VR, D_RET)
    cos_full = cos_ref[...]
    sin_signed = sin_ref[...]
    head_cols = []
    for hh in range(N_HEADS_R):
        cs = slice(hh * HEAD_DIM_R, (hh + 1) * HEAD_DIM_R)
        q_rot = _rotate(zqr[:, cs], cos_full, sin_signed)
        k_rot = _rotate(zkr[:, cs], cos_full, sin_signed) * (HEAD_DIM_R ** -0.5)
        v_b = zvr[:, cs].astype(BF16)
        seqs = []
        for b in range(bb):
            rs = slice(b * dec_seq, (b + 1) * dec_seq)
            q_b = q_rot[rs].astype(BF16)
            k_c = k_rot[rs]
            v_c = v_b[rs]
            state = sin_state_ref[b, hh]
            sc = (_dot_nt(q_b, k_c.astype(BF16)) * dmat_ref[hh]).astype(BF16)
            o = _dot(sc, v_c) + _dot(q_b, state.astype(BF16)) * qdec_ref[hh]
            state_ref[b, hh] = gl_ref[hh] * state + _dot_tn((k_c * kdec_ref[hh]).astype(BF16), v_c)
            seqs.append(o)
        head_cols.append(_group_norm(jnp.concatenate(seqs, axis=0)))
    o_r = jnp.concatenate(head_cols, axis=-1) * gnw_ref[...]
    mixed_r = (o_r * _silu(proj(OFF_GR, D_RET))).astype(BF16)

    y = _dot(mixed_a, w_out_ref[0:D_ATT, :]) + _dot(mixed_r, w_out_ref[D_ATT:, :])
    out = _deepnorm_out(x, gate, y, alpha, lnw_ref[...], lnb_ref[...])
    y_ref[...] = out.reshape(bb, dec_seq, D_MODEL)


def _sample_call(x, cond, cache_k, cache_v, state, sinks, w_in_b, w_out_b, gn_w, ln_w, ln_b, consts, alpha):
    batch, dec_seq, _ = x.shape
    bb = SAMPLE_BATCH_BLOCK
    rows = bb * dec_seq
    n_keys = bb * WINDOW + rows
    smem = pl.BlockSpec(memory_space=pltpu.SMEM)
    whole = lambda shape: pl.BlockSpec(shape, lambda i: (0,) * len(shape))
    return pl.pallas_call(
        functools.partial(_sample_kernel, alpha=alpha, dec_seq=dec_seq),
        out_shape=(
            jax.ShapeDtypeStruct((batch, dec_seq, D_MODEL), F32),
            jax.ShapeDtypeStruct((batch, WINDOW, KV_W), F32),
            jax.ShapeDtypeStruct((batch, WINDOW, KV_W), F32),
            jax.ShapeDtypeStruct((batch, N_HEADS_R, HEAD_DIM_R, HEAD_DIM_R), F32),
        ),
        grid=(batch // bb,),
        in_specs=[
            smem, smem,
            pl.BlockSpec((bb, dec_seq, D_MODEL), lambda i: (i, 0, 0)),
            pl.BlockSpec((bb, 1, 3 * D_MODEL), lambda i: (i, 0, 0)),
            pl.BlockSpec((bb, WINDOW, KV_W), lambda i: (i, 0, 0)),
            pl.BlockSpec((bb, WINDOW, KV_W), lambda i: (i, 0, 0)),
            pl.BlockSpec((bb, N_HEADS_R, HEAD_DIM_R, HEAD_DIM_R), lambda i: (i, 0, 0, 0)),
            whole((D_MODEL, D_IN)),
            whole((D_MODEL, D_MODEL)),
            whole((1, D_RET)),
            whole((1, D_MODEL)),
            whole((1, D_MODEL)),
            whole((rows, HEAD_DIM_R)),
            whole((rows, HEAD_DIM_R)),
            whole((GROUP_A * rows, n_keys)),
            whole((N_HEADS_R, dec_seq, dec_seq)),
            whole((N_HEADS_R, dec_seq, HEAD_DIM_R)),
            whole((N_HEADS_R, dec_seq, HEAD_DIM_R)),
        ],
        out_specs=(
            pl.BlockSpec((bb, dec_seq, D_MODEL), lambda i: (i, 0, 0)),
            pl.BlockSpec((bb, WINDOW, KV_W), lambda i: (i, 0, 0)),
            pl.BlockSpec((bb, WINDOW, KV_W), lambda i: (i, 0, 0)),
            pl.BlockSpec((bb, N_HEADS_R, HEAD_DIM_R, HEAD_DIM_R), lambda i: (i, 0, 0, 0)),
        ),
        compiler_params=pltpu.CompilerParams(
            dimension_semantics=("arbitrary",), vmem_limit_bytes=VMEM_LIMIT_BYTES),
        name="sample_layer",
    )(sinks, consts["gl_dec"], x, cond, cache_k, cache_v, state, w_in_b, w_out_b, gn_w, ln_w, ln_b,
      consts["cos_s"], consts["sin_s"], consts["mask_s"], consts["dmat_s"], consts["qdec_s"], consts["kdec_s"])


def _rope_tables(pos):
    half = HEAD_DIM_R // 2
    inv = ROPE_BASE ** (-np.arange(half, dtype=np.float64) / half)
    ang = pos[:, None].astype(np.float64) * inv[None, :]
    cos, sin = np.cos(ang), np.sin(ang)
    return (np.concatenate([cos, cos], axis=1).astype(np.float32),
            np.concatenate([-sin, sin], axis=1).astype(np.float32))


def _decay_tables(length):
    log_gamma = np.log(1.0 - 2.0 ** (-5.0 - np.arange(N_HEADS_R, dtype=np.float64)))
    idx = np.arange(length, dtype=np.float64)
    diff = idx[:, None] - idx[None, :]
    dmat = np.where(diff >= 0, np.exp(log_gamma[:, None, None] * np.maximum(diff, 0.0)), 0.0)
    qdec = np.exp(log_gamma[:, None] * (idx[None, :] + 1.0))
    kdec = np.exp(log_gamma[:, None] * (length - 1.0 - idx[None, :]))
    lanes = lambda a: np.broadcast_to(a[:, :, None], (N_HEADS_R, length, HEAD_DIM_R)).astype(np.float32)
    return dmat.astype(np.float32), lanes(qdec), lanes(kdec), np.exp(log_gamma * length).astype(np.float32)


def _sample_mask(dec_seq):
    bb = SAMPLE_BATCH_BLOCK
    rows = bb * dec_seq
    r = np.arange(GROUP_A * rows)
    q_seq, q_tok = (r % rows) // dec_seq, r % dec_seq
    c = np.arange(bb * WINDOW)
    cached = (c[None, :] // WINDOW == q_seq[:, None]) & (c[None, :] % WINDOW >= q_tok[:, None])
    n = np.arange(rows)
    fresh = (n[None, :] // dec_seq == q_seq[:, None]) & (n[None, :] % dec_seq <= q_tok[:, None])
    return np.concatenate([cached, fresh], axis=1).astype(np.float32)


def _tables(seq, dec_seq):
    cos_p, sin_p = _rope_tables(np.arange(seq))
    cos_s, sin_s = _rope_tables(PAST_LEN + np.arange(dec_seq))
    dmat, qdec, kdec, gl_chunk = _decay_tables(CHUNK_R)
    dmat_s, qdec_s, kdec_s, gl_dec = _decay_tables(dec_seq)
    tile = lambda a: np.tile(a, (SAMPLE_BATCH_BLOCK, 1))
    return dict(cos_p=cos_p, sin_p=sin_p, cos_s=tile(cos_s), sin_s=tile(sin_s), mask_s=_sample_mask(dec_seq),
                dmat=dmat, qdec=qdec, kdec=kdec, gl_chunk=gl_chunk,
                dmat_s=dmat_s, qdec_s=qdec_s, kdec_s=kdec_s, gl_dec=gl_dec)


def kernel(x_prompt, x_sample, c_prompt, c_sample, cache_k_win, cache_v_win, state_ret,
           w_ada, b_ada, w_in, attn_sinks, ret_gn_w, w_out, ln_w, ln_b):
    depth = w_in.shape[0]
    batch, seq, _ = x_prompt.shape
    dec_batch, dec_seq, _ = x_sample.shape
    assert seq % PROMPT_BLOCK == 0 and batch % PROMPT_BATCH_BLOCK == 0
    assert dec_batch % SAMPLE_BATCH_BLOCK == 0 and dec_seq == 8
    alpha = float((2 * depth) ** 0.25)
    consts = {k: jnp.asarray(v) for k, v in _tables(seq, dec_seq).items()}
    c_all = jnp.concatenate([c_prompt, c_sample], axis=0)
    x_p, x_s = x_prompt, x_sample
    kp, vp, sp, ks, vs, ss = [], [], [], [], [], []
    for l in range(depth):
        cond = _cond_call(c_all, w_ada[l], b_ada[l])
        w_in_b = w_in[l].astype(BF16)
        w_out_b = w_out[l].astype(BF16)
        gn_w = ret_gn_w[l].reshape(1, D_RET)
        lw = ln_w[l].reshape(1, D_MODEL)
        lb = ln_b[l].reshape(1, D_MODEL)
        x_p, k_p, v_p, s_p = _prompt_call(
            x_p, cond[:batch].reshape(batch, 1, 3 * D_MODEL), attn_sinks[l], w_in_b, w_out_b, gn_w, lw, lb,
            consts, alpha)
        x_s, k_s, v_s, s_s = _sample_call(
            x_s, cond[batch:].reshape(dec_batch, 1, 3 * D_MODEL),
            cache_k_win[l].reshape(dec_batch, WINDOW, KV_W), cache_v_win[l].reshape(dec_batch, WINDOW, KV_W),
            state_ret[l], attn_sinks[l], w_in_b, w_out_b, gn_w, lw, lb, consts, alpha)
        kp.append(k_p.reshape(batch, WINDOW, N_KV_HEADS_A, HEAD_DIM_A))
        vp.append(v_p.reshape(batch, WINDOW, N_KV_HEADS_A, HEAD_DIM_A))
        sp.append(s_p)
        ks.append(k_s.reshape(dec_batch, WINDOW, N_KV_HEADS_A, HEAD_DIM_A))
        vs.append(v_s.reshape(dec_batch, WINDOW, N_KV_HEADS_A, HEAD_DIM_A))
        ss.append(s_s)
    return (x_p, x_s, jnp.stack(kp), jnp.stack(vp), jnp.stack(sp), jnp.stack(ks), jnp.stack(vs), jnp.stack(ss))
```

```python
import functools

import jax
import jax.numpy as jnp
import numpy as np
from jax import lax
from jax.experimental import pallas as pl
from jax.experimental.pallas import tpu as pltpu

D_MODEL = 1024
D_ATT = 512
D_RET = 512
HEAD_DIM_A = 64
N_HEADS_A = 8
N_KV_HEADS_A = 2
GROUP_A = 4
KV_W = 128
WINDOW = 128
N_HEADS_R = 4
HEAD_DIM_R = 128
CHUNK_R = 128
ROPE_BASE = 10000.0
LN_EPS = 1e-5
GN_EPS = 1e-5
PAST_LEN = 16384
MASKED = -1e30

OFF_QA = 0
OFF_KA = OFF_QA + D_ATT
OFF_VA = OFF_KA + KV_W
OFF_GA = OFF_VA + KV_W
OFF_QR = OFF_GA + D_ATT
OFF_KR = OFF_QR + D_RET
OFF_VR = OFF_KR + D_RET
OFF_GR = OFF_VR + D_RET
D_IN = OFF_GR + D_RET

PROMPT_BLOCK = 256
PROMPT_BATCH_BLOCK = 4
PROMPT_STAGE_SKEW = 9
DENSE_PIECE = 256
SAMPLE_BATCH_BLOCK = 8
COND_COL_BLOCK = 512
VMEM_LIMIT_BYTES = 48 * 1024 * 1024

F32 = jnp.float32
BF16 = jnp.bfloat16


def _silu(x):
    return x * jax.nn.sigmoid(x)


def _dot(a, b):
    return jnp.dot(a, b, preferred_element_type=F32)


def _dot_nt(a, b):
    return lax.dot_general(a, b, (((1,), (1,)), ((), ())), preferred_element_type=F32)


def _dot_tn(a, b):
    return lax.dot_general(a, b, (((0,), (0,)), ((), ())), preferred_element_type=F32)


def _rotate(x, cos_full, sin_signed):
    return x * cos_full + pltpu.roll(x, HEAD_DIM_R // 2, 1) * sin_signed


def _group_norm(o):
    mu = jnp.mean(o, axis=-1, keepdims=True)
    d = o - mu
    var = jnp.mean(d * d, axis=-1, keepdims=True)
    return d * lax.rsqrt(var + GN_EPS)


def _deepnorm_out(x, gate, y, alpha, ln_w, ln_b):
    r = alpha * x + gate * y
    mu = jnp.mean(r, axis=-1, keepdims=True)
    d = r - mu
    var = jnp.mean(d * d, axis=-1, keepdims=True)
    return d * lax.rsqrt(var + LN_EPS) * ln_w + ln_b


def _sink_softmax(s, sink, axis=-1):
    m = jnp.maximum(jnp.max(s, axis=axis, keepdims=True), sink)
    p = jnp.exp(s - m)
    denom = jnp.sum(p, axis=axis, keepdims=True) + jnp.exp(sink - m)
    return p.astype(BF16), denom


def _cond_kernel(c_ref, w_ref, b_ref, o_ref):
    a = _silu(c_ref[...]).astype(BF16)
    o_ref[...] = _dot(a, w_ref[...].astype(BF16)) + b_ref[...]


def _cond_call(c_all, w_ada, b_ada):
    rows = c_all.shape[0]
    n_out = w_ada.shape[1]
    return pl.pallas_call(
        _cond_kernel,
        out_shape=jax.ShapeDtypeStruct((rows, n_out), F32),
        grid=(n_out // COND_COL_BLOCK,),
        in_specs=[
            pl.BlockSpec((rows, D_MODEL), lambda j: (0, 0)),
            pl.BlockSpec((D_MODEL, COND_COL_BLOCK), lambda j: (0, j)),
            pl.BlockSpec((1, COND_COL_BLOCK), lambda j: (0, j)),
        ],
        out_specs=pl.BlockSpec((rows, COND_COL_BLOCK), lambda j: (0, j)),
        compiler_params=pltpu.CompilerParams(dimension_semantics=("arbitrary",)),
        name="adaln_cond",
    )(c_all, w_ada, b_ada.reshape(1, n_out))


def _prompt_kernel(sinks_ref, gl_ref, x_ref, cond_ref, w_in_ref, w_out_ref, gnw_ref, lnw_ref, lnb_ref,
                   cos_ref, sin_ref, dmat_ref, qdec_ref, kdec_ref,
                   y_ref, kwin_ref, vwin_ref, state_ref, kprev_ref, vprev_ref, *, alpha):
    t = pl.program_id(1)

    @pl.when(t == 0)
    def _():
        kprev_ref[...] = jnp.zeros_like(kprev_ref)
        vprev_ref[...] = jnp.zeros_like(vprev_ref)
        state_ref[...] = jnp.zeros_like(state_ref)

    stages = [_prompt_sequence_block(bi, t, sinks_ref, gl_ref, x_ref, cond_ref, w_in_ref, w_out_ref, gnw_ref,
                                     lnw_ref, lnb_ref, cos_ref, sin_ref, dmat_ref, qdec_ref, kdec_ref,
                                     y_ref, kwin_ref, vwin_ref, state_ref, kprev_ref, vprev_ref, alpha)
              for bi in range(PROMPT_BATCH_BLOCK)]
    live = list(range(PROMPT_BATCH_BLOCK))
    tick = 0
    while live:
        for bi in list(live):
            if tick >= bi * PROMPT_STAGE_SKEW and next(stages[bi], "done") == "done":
                live.remove(bi)
        tick += 1


def _prompt_sequence_block(bi, t, sinks_ref, gl_ref, x_ref, cond_ref, w_in_ref, w_out_ref, gnw_ref, lnw_ref, lnb_ref,
                           cos_ref, sin_ref, dmat_ref, qdec_ref, kdec_ref,
                           y_ref, kwin_ref, vwin_ref, state_ref, kprev_ref, vprev_ref, alpha):
    n_sub = PROMPT_BLOCK // WINDOW
    x = x_ref[bi]
    shift = cond_ref[bi, :, 0:D_MODEL]
    scale = cond_ref[bi, :, D_MODEL:2 * D_MODEL]
    gate = cond_ref[bi, :, 2 * D_MODEL:3 * D_MODEL]
    h = (x * (1.0 + scale) + shift).astype(BF16)

    def proj(off, width):
        return _dot(h, w_in_ref[:, off:off + width])

    pending = [(name, off + c)
               for name, off, width in (("ga", OFF_GA, D_ATT), ("qr", OFF_QR, D_RET), ("kr", OFF_KR, D_RET),
                                        ("vr", OFF_VR, D_RET), ("gr", OFF_GR, D_RET))
               for c in range(0, width, DENSE_PIECE)]
    n_pending = len(pending)
    pieces = {}

    def emit_dense(n):
        for _ in range(min(n, len(pending))):
            name, off = pending.pop(0)
            pieces.setdefault(name, []).append(proj(off, DENSE_PIECE))

    def section(name):
        return jnp.concatenate(pieces[name], axis=-1)

    zkv = proj(OFF_KA, 2 * KV_W)
    zk = zkv[:, 0:KV_W]
    zv = zkv[:, KV_W:2 * KV_W]

    kwin_ref[bi] = zk[PROMPT_BLOCK - WINDOW:]
    vwin_ref[bi] = zv[PROMPT_BLOCK - WINDOW:]

    zk_b = zk.astype(BF16)
    zv_b = zv.astype(BF16)
    yield
    zq = (jnp.concatenate([proj(OFF_QA + c, DENSE_PIECE) for c in range(0, D_ATT, DENSE_PIECE)], axis=-1)
          * (HEAD_DIM_A ** -0.5)).astype(BF16)
    yield

    kj = lax.broadcasted_iota(jnp.int32, (2 * WINDOW, WINDOW), 0)
    qi = lax.broadcasted_iota(jnp.int32, (2 * WINDOW, WINDOW), 1)
    band = (kj >= qi) & (kj <= qi + WINDOW)

    units = []
    for s in range(n_sub):
        r0 = s * WINDOW
        k_prev = kprev_ref[bi] if s == 0 else zk_b[r0 - WINDOW:r0]
        v_prev = vprev_ref[bi] if s == 0 else zv_b[r0 - WINDOW:r0]
        kk = jnp.concatenate([k_prev, zk_b[r0:r0 + WINDOW]], axis=0)
        vv = jnp.concatenate([v_prev, zv_b[r0:r0 + WINDOW]], axis=0)
        start = t * PROMPT_BLOCK + r0
        valid = band & (kj >= WINDOW - start)
        for hk in range(N_KV_HEADS_A):
            kh = kk[:, hk * HEAD_DIM_A:(hk + 1) * HEAD_DIM_A]
            vh = vv[:, hk * HEAD_DIM_A:(hk + 1) * HEAD_DIM_A]
            qs = jnp.concatenate(
                [zq[r0:r0 + WINDOW, (hk * GROUP_A + g) * HEAD_DIM_A:(hk * GROUP_A + g + 1) * HEAD_DIM_A]
                 for g in range(GROUP_A)], axis=0)
            units.append((hk, valid, _dot_nt(kh, qs), vh))
    kprev_ref[bi] = zk_b[PROMPT_BLOCK - WINDOW:]
    vprev_ref[bi] = zv_b[PROMPT_BLOCK - WINDOW:]
    yield

    fill = -(-n_pending // len(units))
    softmaxed = []
    for hk, valid, sc, vh in units:
        ps, denoms = [], []
        for g in range(GROUP_A):
            sg = jnp.where(valid, sc[:, g * WINDOW:(g + 1) * WINDOW], MASKED)
            p, denom = _sink_softmax(sg, sinks_ref[hk * GROUP_A + g], axis=0)
            ps.append(p)
            denoms.append(denom)
        softmaxed.append((jnp.concatenate(ps, axis=-1), jnp.concatenate(denoms, axis=-1), vh))
        emit_dense(fill)
        yield
    emit_dense(len(pending))

    tiles = []
    for p, denom, vh in softmaxed:
        o_t = _dot_tn(vh, p) / denom
        for pr in range(GROUP_A // 2):
            pair = jnp.concatenate([o_t[:, (2 * pr + e) * WINDOW:(2 * pr + e + 1) * WINDOW] for e in range(2)], axis=0)
            tiles.append(pair.T)
    yield
    tiles_per_sub = N_HEADS_A // 2
    o_a = jnp.concatenate(
        [jnp.concatenate(tiles[s * tiles_per_sub:(s + 1) * tiles_per_sub], axis=-1) for s in range(n_sub)], axis=0)
    mixed_a = (o_a * _silu(section("ga"))).astype(BF16)
    yield

    zqr, zkr, zvr = section("qr"), section("kr"), section("vr")
    cos_full = cos_ref[...]
    sin_signed = sin_ref[...]
    chunk_rows = [slice(s * CHUNK_R, (s + 1) * CHUNK_R) for s in range(n_sub)]
    q_b, v_b, raw, kv, y_a = [], [], [], [], []
    for hh in range(N_HEADS_R):
        cs = slice(hh * HEAD_DIM_R, (hh + 1) * HEAD_DIM_R)
        q_h = _rotate(zqr[:, cs], cos_full, sin_signed).astype(BF16)
        k_h = _rotate(zkr[:, cs], cos_full, sin_signed) * (HEAD_DIM_R ** -0.5)
        v_h = zvr[:, cs].astype(BF16)
        k_hb = k_h.astype(BF16)
        q_b.append(q_h)
        v_b.append(v_h)
        raw.append([_dot_nt(q_h[rs], k_hb[rs]) for rs in chunk_rows])
        kv.append([_dot_tn((k_h[rs] * kdec_ref[hh]).astype(BF16), v_h[rs]) for rs in chunk_rows])
        cols = slice(hh * DENSE_PIECE, (hh + 1) * DENSE_PIECE)
        y_a.append(_dot(mixed_a, w_out_ref[0:D_ATT, cols]))
        yield
    head_cols = []
    for hh in range(N_HEADS_R):
        state = state_ref[bi, hh]
        chunks = []
        for s, rs in enumerate(chunk_rows):
            sc = (raw[hh][s] * dmat_ref[hh]).astype(BF16)
            chunks.append(_dot(sc, v_b[hh][rs]) + _dot(q_b[hh][rs], state.astype(BF16)) * qdec_ref[hh])
            state = gl_ref[hh] * state + kv[hh][s]
        state_ref[bi, hh] = state
        head_cols.append(_group_norm(jnp.concatenate(chunks, axis=0)))
        yield
    o_r = jnp.concatenate(head_cols, axis=-1) * gnw_ref[...]
    mixed_r = (o_r * _silu(section("gr"))).astype(BF16)
    yield

    y = jnp.concatenate(y_a, axis=-1) + _dot(mixed_r, w_out_ref[D_ATT:, :])
    y_ref[bi] = _deepnorm_out(x, gate, y, alpha, lnw_ref[...], lnb_ref[...])


def _prompt_call(x, cond, sinks, w_in_b, w_out_b, gn_w, ln_w, ln_b, consts, alpha):
    batch, seq, _ = x.shape
    n_t = seq // PROMPT_BLOCK
    nb = PROMPT_BATCH_BLOCK
    smem = pl.BlockSpec(memory_space=pltpu.SMEM)
    whole = lambda shape: pl.BlockSpec(shape, lambda b, t: (0,) * len(shape))
    return pl.pallas_call(
        functools.partial(_prompt_kernel, alpha=alpha),
        out_shape=(
            jax.ShapeDtypeStruct((batch, seq, D_MODEL), F32),
            jax.ShapeDtypeStruct((batch, WINDOW, KV_W), F32),
            jax.ShapeDtypeStruct((batch, WINDOW, KV_W), F32),
            jax.ShapeDtypeStruct((batch, N_HEADS_R, HEAD_DIM_R, HEAD_DIM_R), F32),
        ),
        grid=(batch // nb, n_t),
        in_specs=[
            smem, smem,
            pl.BlockSpec((nb, PROMPT_BLOCK, D_MODEL), lambda b, t: (b, t, 0)),
            pl.BlockSpec((nb, 1, 3 * D_MODEL), lambda b, t: (b, 0, 0)),
            whole((D_MODEL, D_IN)),
            whole((D_MODEL, D_MODEL)),
            whole((1, D_RET)),
            whole((1, D_MODEL)),
            whole((1, D_MODEL)),
            pl.BlockSpec((PROMPT_BLOCK, HEAD_DIM_R), lambda b, t: (t, 0)),
            pl.BlockSpec((PROMPT_BLOCK, HEAD_DIM_R), lambda b, t: (t, 0)),
            whole((N_HEADS_R, CHUNK_R, CHUNK_R)),
            whole((N_HEADS_R, CHUNK_R, HEAD_DIM_R)),
            whole((N_HEADS_R, CHUNK_R, HEAD_DIM_R)),
        ],
        out_specs=(
            pl.BlockSpec((nb, PROMPT_BLOCK, D_MODEL), lambda b, t: (b, t, 0)),
            pl.BlockSpec((nb, WINDOW, KV_W), lambda b, t: (b, 0, 0)),
            pl.BlockSpec((nb, WINDOW, KV_W), lambda b, t: (b, 0, 0)),
            pl.BlockSpec((nb, N_HEADS_R, HEAD_DIM_R, HEAD_DIM_R), lambda b, t: (b, 0, 0, 0)),
        ),
        scratch_shapes=[pltpu.VMEM((nb, WINDOW, KV_W), BF16), pltpu.VMEM((nb, WINDOW, KV_W), BF16)],
        compiler_params=pltpu.CompilerParams(
            dimension_semantics=("arbitrary", "arbitrary"), vmem_limit_bytes=VMEM_LIMIT_BYTES),
        name="prompt_layer",
    )(sinks, consts["gl_chunk"], x, cond, w_in_b, w_out_b, gn_w, ln_w, ln_b,
      consts["cos_p"], consts["sin_p"], consts["dmat"], consts["qdec"], consts["kdec"])


def _sample_kernel(sinks_ref, gl_ref, x_ref, cond_ref, ck_ref, cv_ref, sin_state_ref, w_in_ref, w_out_ref,
                   gnw_ref, lnw_ref, lnb_ref, cos_ref, sin_ref, mask_ref, dmat_ref, qdec_ref, kdec_ref,
                   y_ref, kwin_ref, vwin_ref, state_ref, *, alpha, dec_seq):
    bb = SAMPLE_BATCH_BLOCK
    rows = bb * dec_seq
    x3 = x_ref[...]
    shift = cond_ref[:, :, 0:D_MODEL]
    scale = cond_ref[:, :, D_MODEL:2 * D_MODEL]
    gate3 = cond_ref[:, :, 2 * D_MODEL:3 * D_MODEL]
    h = (x3 * (1.0 + scale) + shift).reshape(rows, D_MODEL).astype(BF16)
    x = x3.reshape(rows, D_MODEL)
    gate = jnp.broadcast_to(gate3, (bb, dec_seq, D_MODEL)).reshape(rows, D_MODEL)

    def proj(off, width):
        return _dot(h, w_in_ref[:, off:off + width])

    zk = proj(OFF_KA, KV_W)
    zv = proj(OFF_VA, KV_W)
    ck = ck_ref[...]
    cv = cv_ref[...]
    kwin_ref[:, 0:WINDOW - dec_seq, :] = ck[:, dec_seq:, :]
    vwin_ref[:, 0:WINDOW - dec_seq, :] = cv[:, dec_seq:, :]
    kwin_ref[:, WINDOW - dec_seq:, :] = zk.reshape(bb, dec_seq, KV_W)
    vwin_ref[:, WINDOW - dec_seq:, :] = zv.reshape(bb, dec_seq, KV_W)
    keys = jnp.concatenate([ck.reshape(bb * WINDOW, KV_W).astype(BF16), zk.astype(BF16)], axis=0)
    vals = jnp.concatenate([cv.reshape(bb * WINDOW, KV_W).astype(BF16), zv.astype(BF16)], axis=0)
    zq = (proj(OFF_QA, D_ATT) * (HEAD_DIM_A ** -0.5)).astype(BF16)
    valid = mask_ref[...] > 0.5
    row_idx = lax.broadcasted_iota(jnp.int32, (GROUP_A * rows, 1), 0)
    heads = []
    for hk in range(N_KV_HEADS_A):
        kh = keys[:, hk * HEAD_DIM_A:(hk + 1) * HEAD_DIM_A]
        vh = vals[:, hk * HEAD_DIM_A:(hk + 1) * HEAD_DIM_A]
        qs = jnp.concatenate(
            [zq[:, (hk * GROUP_A + g) * HEAD_DIM_A:(hk * GROUP_A + g + 1) * HEAD_DIM_A] for g in range(GROUP_A)],
            axis=0)
        sink = jnp.full((GROUP_A * rows, 1), sinks_ref[hk * GROUP_A], F32)
        for g in range(1, GROUP_A):
            sink = jnp.where(row_idx >= g * rows, sinks_ref[hk * GROUP_A + g], sink)
        sc = jnp.where(valid, _dot_nt(qs, kh), MASKED)
        p, denom = _sink_softmax(sc, sink)
        o = _dot(p, vh) / denom
        for g in range(GROUP_A):
            heads.append(o[g * rows:(g + 1) * rows])
    o_a = jnp.concatenate(heads, axis=-1)
    mixed_a = (o_a * _silu(proj(OFF_GA, D_ATT))).astype(BF16)

    zqr = proj(OFF_QR, D_RET)
    zkr = proj(OFF_KR, D_RET)
    zvr = proj(OFF_VR, D_RET)
    cos_full = cos_ref[...]
    sin_signed = sin_ref[...]
    head_cols = []
    for hh in range(N_HEADS_R):
        cs = slice(hh * HEAD_DIM_R, (hh + 1) * HEAD_DIM_R)
        q_rot = _rotate(zqr[:, cs], cos_full, sin_signed)
        k_rot = _rotate(zkr[:, cs], cos_full, sin_signed) * (HEAD_DIM_R ** -0.5)
        v_b = zvr[:, cs].astype(BF16)
        seqs = []
        for b in range(bb):
            rs = slice(b * dec_seq, (b + 1) * dec_seq)
            q_b = q_rot[rs].astype(BF16)
            k_c = k_rot[rs]
            v_c = v_b[rs]
            state = sin_state_ref[b, hh]
            sc = (_dot_nt(q_b, k_c.astype(BF16)) * dmat_ref[hh]).astype(BF16)
            o = _dot(sc, v_c) + _dot(q_b, state.astype(BF16)) * qdec_ref[hh]
            state_ref[b, hh] = gl_ref[hh] * state + _dot_tn((k_c * kdec_ref[hh]).astype(BF16), v_c)
            seqs.append(o)
        head_cols.append(_group_norm(jnp.concatenate(seqs, axis=0)))
    o_r = jnp.concatenate(head_cols, axis=-1) * gnw_ref[...]
    mixed_r = (o_r * _silu(proj(OFF_GR, D_RET))).astype(BF16)

    y = _dot(mixed_a, w_out_ref[0:D_ATT, :]) + _dot(mixed_r, w_out_ref[D_ATT:, :])
    out = _deepnorm_out(x, gate, y, alpha, lnw_ref[...], lnb_ref[...])
    y_ref[...] = out.reshape(bb, dec_seq, D_MODEL)


def _sample_call(x, cond, cache_k, cache_v, state, sinks, w_in_b, w_out_b, gn_w, ln_w, ln_b, consts, alpha):
    batch, dec_seq, _ = x.shape
    bb = SAMPLE_BATCH_BLOCK
    rows = bb * dec_seq
    n_keys = bb * WINDOW + rows
    smem = pl.BlockSpec(memory_space=pltpu.SMEM)
    whole = lambda shape: pl.BlockSpec(shape, lambda i: (0,) * len(shape))
    return pl.pallas_call(
        functools.partial(_sample_kernel, alpha=alpha, dec_seq=dec_seq),
        out_shape=(
            jax.ShapeDtypeStruct((batch, dec_seq, D_MODEL), F32),
            jax.ShapeDtypeStruct((batch, WINDOW, KV_W), F32),
            jax.ShapeDtypeStruct((batch, WINDOW, KV_W), F32),
            jax.ShapeDtypeStruct((batch, N_HEADS_R, HEAD_DIM_R, HEAD_DIM_R), F32),
        ),
        grid=(batch // bb,),
        in_specs=[
            smem, smem,
            pl.BlockSpec((bb, dec_seq, D_MODEL), lambda i: (i, 0, 0)),
            pl.BlockSpec((bb, 1, 3 * D_MODEL), lambda i: (i, 0, 0)),
            pl.BlockSpec((bb, WINDOW, KV_W), lambda i: (i, 0, 0)),
            pl.BlockSpec((bb, WINDOW, KV_W), lambda i: (i, 0, 0)),
            pl.BlockSpec((bb, N_HEADS_R, HEAD_DIM_R, HEAD_DIM_R), lambda i: (i, 0, 0, 0)),
            whole((D_MODEL, D_IN)),
            whole((D_MODEL, D_MODEL)),
            whole((1, D_RET)),
            whole((1, D_MODEL)),
            whole((1, D_MODEL)),
            whole((rows, HEAD_DIM_R)),
            whole((rows, HEAD_DIM_R)),
            whole((GROUP_A * rows, n_keys)),
            whole((N_HEADS_R, dec_seq, dec_seq)),
            whole((N_HEADS_R, dec_seq, HEAD_DIM_R)),
            whole((N_HEADS_R, dec_seq, HEAD_DIM_R)),
        ],
        out_specs=(
            pl.BlockSpec((bb, dec_seq, D_MODEL), lambda i: (i, 0, 0)),
            pl.BlockSpec((bb, WINDOW, KV_W), lambda i: (i, 0, 0)),
            pl.BlockSpec((bb, WINDOW, KV_W), lambda i: (i, 0, 0)),
            pl.BlockSpec((bb, N_HEADS_R, HEAD_DIM_R, HEAD_DIM_R), lambda i: (i, 0, 0, 0)),
        ),
        compiler_params=pltpu.CompilerParams(
            dimension_semantics=("arbitrary",), vmem_limit_bytes=VMEM_LIMIT_BYTES),
        name="sample_layer",
    )(sinks, consts["gl_dec"], x, cond, cache_k, cache_v, state, w_in_b, w_out_b, gn_w, ln_w, ln_b,
      consts["cos_s"], consts["sin_s"], consts["mask_s"], consts["dmat_s"], consts["qdec_s"], consts["kdec_s"])


def _rope_tables(pos):
    half = HEAD_DIM_R // 2
    inv = ROPE_BASE ** (-np.arange(half, dtype=np.float64) / half)
    ang = pos[:, None].astype(np.float64) * inv[None, :]
    cos, sin = np.cos(ang), np.sin(ang)
    return (np.concatenate([cos, cos], axis=1).astype(np.float32),
            np.concatenate([-sin, sin], axis=1).astype(np.float32))


def _decay_tables(length):
    log_gamma = np.log(1.0 - 2.0 ** (-5.0 - np.arange(N_HEADS_R, dtype=np.float64)))
    idx = np.arange(length, dtype=np.float64)
    diff = idx[:, None] - idx[None, :]
    dmat = np.where(diff >= 0, np.exp(log_gamma[:, None, None] * np.maximum(diff, 0.0)), 0.0)
    qdec = np.exp(log_gamma[:, None] * (idx[None, :] + 1.0))
    kdec = np.exp(log_gamma[:, None] * (length - 1.0 - idx[None, :]))
    lanes = lambda a: np.broadcast_to(a[:, :, None], (N_HEADS_R, length, HEAD_DIM_R)).astype(np.float32)
    return dmat.astype(np.float32), lanes(qdec), lanes(kdec), np.exp(log_gamma * length).astype(np.float32)


def _sample_mask(dec_seq):
    bb = SAMPLE_BATCH_BLOCK
    rows = bb * dec_seq
    r = np.arange(GROUP_A * rows)
    q_seq, q_tok = (r % rows) // dec_seq, r % dec_seq
    c = np.arange(bb * WINDOW)
    cached = (c[None, :] // WINDOW == q_seq[:, None]) & (c[None, :] % WINDOW >= q_tok[:, None])
    n = np.arange(rows)
    fresh = (n[None, :] // dec_seq == q_seq[:, None]) & (n[None, :] % dec_seq <= q_tok[:, None])
    return np.concatenate([cached, fresh], axis=1).astype(np.float32)


def _tables(seq, dec_seq):
    cos_p, sin_p = _rope_tables(np.arange(seq))
    cos_s, sin_s = _rope_tables(PAST_LEN + np.arange(dec_seq))
    dmat, qdec, kdec, gl_chunk = _decay_tables(CHUNK_R)
    dmat_s, qdec_s, kdec_s, gl_dec = _decay_tables(dec_seq)
    tile = lambda a: np.tile(a, (SAMPLE_BATCH_BLOCK, 1))
    return dict(cos_p=cos_p, sin_p=sin_p, cos_s=tile(cos_s), sin_s=tile(sin_s), mask_s=_sample_mask(dec_seq),
                dmat=dmat, qdec=qdec, kdec=kdec, gl_chunk=gl_chunk,
                dmat_s=dmat_s, qdec_s=qdec_s, kdec_s=kdec_s, gl_dec=gl_dec)


def kernel(x_prompt, x_sample, c_prompt, c_sample, cache_k_win, cache_v_win, state_ret,
           w_ada, b_ada, w_in, attn_sinks, ret_gn_w, w_out, ln_w, ln_b):
    depth = w_in.shape[0]
    batch, seq, _ = x_prompt.shape
    dec_batch, dec_seq, _ = x_sample.shape
    assert seq % PROMPT_BLOCK == 0 and batch % PROMPT_BATCH_BLOCK == 0
    assert dec_batch % SAMPLE_BATCH_BLOCK == 0 and dec_seq == 8
    alpha = float((2 * depth) ** 0.25)
    consts = {k: jnp.asarray(v) for k, v in _tables(seq, dec_seq).items()}
    c_all = jnp.concatenate([c_prompt, c_sample], axis=0)
    x_p, x_s = x_prompt, x_sample
    kp, vp, sp, ks, vs, ss = [], [], [], [], [], []
    for l in range(depth):
        cond = _cond_call(c_all, w_ada[l], b_ada[l])
        w_in_b = w_in[l].astype(BF16)
        w_out_b = w_out[l].astype(BF16)
        gn_w = ret_gn_w[l].reshape(1, D_RET)
        lw = ln_w[l].reshape(1, D_MODEL)
        lb = ln_b[l].reshape(1, D_MODEL)
        x_p, k_p, v_p, s_p = _prompt_call(
            x_p, cond[:batch].reshape(batch, 1, 3 * D_MODEL), attn_sinks[l], w_in_b, w_out_b, gn_w, lw, lb,
            consts, alpha)
        x_s, k_s, v_s, s_s = _sample_call(
            x_s, cond[batch:].reshape(dec_batch, 1, 3 * D_MODEL),
            cache_k_win[l].reshape(dec_batch, WINDOW, KV_W), cache_v_win[l].reshape(dec_batch, WINDOW, KV_W),
            state_ret[l], attn_sinks[l], w_in_b, w_out_b, gn_w, lw, lb, consts, alpha)
        kp.append(k_p.reshape(batch, WINDOW, N_KV_HEADS_A, HEAD_DIM_A))
        vp.append(v_p.reshape(batch, WINDOW, N_KV_HEADS_A, HEAD_DIM_A))
        sp.append(s_p)
        ks.append(k_s.reshape(dec_batch, WINDOW, N_KV_HEADS_A, HEAD_DIM_A))
        vs.append(v_s.reshape(dec_batch, WINDOW, N_KV_HEADS_A, HEAD_DIM_A))
        ss.append(s_s)
    return (x_p, x_s, jnp.stack(kp), jnp.stack(vp), jnp.stack(sp), jnp.stack(ks), jnp.stack(vs), jnp.stack(ss))
```

```python
import functools

import jax
import jax.numpy as jnp
import numpy as np
from jax import lax
from jax.experimental import pallas as pl
from jax.experimental.pallas import tpu as pltpu

D_MODEL = 1024
D_ATT = 512
D_RET = 512
HEAD_DIM_A = 64
N_HEADS_A = 8
N_KV_HEADS_A = 2
GROUP_A = 4
KV_W = 128
WINDOW = 128
N_HEADS_R = 4
HEAD_DIM_R = 128
CHUNK_R = 128
ROPE_BASE = 10000.0
LN_EPS = 1e-5
GN_EPS = 1e-5
PAST_LEN = 16384
MASKED = -1e30

OFF_QA = 0
OFF_KA = OFF_QA + D_ATT
OFF_VA = OFF_KA + KV_W
OFF_GA = OFF_VA + KV_W
OFF_QR = OFF_GA + D_ATT
OFF_KR = OFF_QR + D_RET
OFF_VR = OFF_KR + D_RET
OFF_GR = OFF_VR + D_RET
D_IN = OFF_GR + D_RET

PROMPT_BLOCK = 256
PROMPT_BATCH_BLOCK = 4
PROMPT_STAGE_SKEW = 9
DENSE_PIECE = 256
SAMPLE_BATCH_BLOCK = 16
SAMPLE_ATT_GROUP = 4
COND_COL_BLOCK = 512
VMEM_LIMIT_BYTES = 48 * 1024 * 1024

F32 = jnp.float32
BF16 = jnp.bfloat16


def _silu(x):
    return x * jax.nn.sigmoid(x)


def _dot(a, b):
    return jnp.dot(a, b, preferred_element_type=F32)


def _dot_nt(a, b):
    return lax.dot_general(a, b, (((1,), (1,)), ((), ())), preferred_element_type=F32)


def _dot_tn(a, b):
    return lax.dot_general(a, b, (((0,), (0,)), ((), ())), preferred_element_type=F32)


def _rotate(x, cos_full, sin_signed):
    return x * cos_full + pltpu.roll(x, HEAD_DIM_R // 2, 1) * sin_signed


def _group_norm(o):
    mu = jnp.mean(o, axis=-1, keepdims=True)
    d = o - mu
    var = jnp.mean(d * d, axis=-1, keepdims=True)
    return d * lax.rsqrt(var + GN_EPS)


def _deepnorm_out(x, gate, y, alpha, ln_w, ln_b):
    r = alpha * x + gate * y
    mu = jnp.mean(r, axis=-1, keepdims=True)
    d = r - mu
    var = jnp.mean(d * d, axis=-1, keepdims=True)
    return d * lax.rsqrt(var + LN_EPS) * ln_w + ln_b


def _sink_softmax(s, sink, axis=-1):
    m = jnp.maximum(jnp.max(s, axis=axis, keepdims=True), sink)
    p = jnp.exp(s - m)
    denom = jnp.sum(p, axis=axis, keepdims=True) + jnp.exp(sink - m)
    return p.astype(BF16), denom


def _cond_kernel(c_ref, w_ref, b_ref, o_ref):
    a = _silu(c_ref[...]).astype(BF16)
    o_ref[...] = _dot(a, w_ref[...].astype(BF16)) + b_ref[...]


def _cond_call(c_all, w_ada, b_ada):
    rows = c_all.shape[0]
    n_out = w_ada.shape[1]
    return pl.pallas_call(
        _cond_kernel,
        out_shape=jax.ShapeDtypeStruct((rows, n_out), F32),
        grid=(n_out // COND_COL_BLOCK,),
        in_specs=[
            pl.BlockSpec((rows, D_MODEL), lambda j: (0, 0)),
            pl.BlockSpec((D_MODEL, COND_COL_BLOCK), lambda j: (0, j)),
            pl.BlockSpec((1, COND_COL_BLOCK), lambda j: (0, j)),
        ],
        out_specs=pl.BlockSpec((rows, COND_COL_BLOCK), lambda j: (0, j)),
        compiler_params=pltpu.CompilerParams(dimension_semantics=("arbitrary",)),
        name="adaln_cond",
    )(c_all, w_ada, b_ada.reshape(1, n_out))


def _prompt_kernel(sinks_ref, gl_ref, x_ref, cond_ref, w_in_ref, w_out_ref, gnw_ref, lnw_ref, lnb_ref,
                   cos_ref, sin_ref, dmat_ref, qdec_ref, kdec_ref,
                   y_ref, kwin_ref, vwin_ref, state_ref, kprev_ref, vprev_ref, *, alpha):
    t = pl.program_id(1)

    @pl.when(t == 0)
    def _():
        kprev_ref[...] = jnp.zeros_like(kprev_ref)
        vprev_ref[...] = jnp.zeros_like(vprev_ref)
        state_ref[...] = jnp.zeros_like(state_ref)

    stages = [_prompt_sequence_block(bi, t, sinks_ref, gl_ref, x_ref, cond_ref, w_in_ref, w_out_ref, gnw_ref,
                                     lnw_ref, lnb_ref, cos_ref, sin_ref, dmat_ref, qdec_ref, kdec_ref,
                                     y_ref, kwin_ref, vwin_ref, state_ref, kprev_ref, vprev_ref, alpha)
              for bi in range(PROMPT_BATCH_BLOCK)]
    live = list(range(PROMPT_BATCH_BLOCK))
    tick = 0
    while live:
        for bi in list(live):
            if tick >= bi * PROMPT_STAGE_SKEW and next(stages[bi], "done") == "done":
                live.remove(bi)
        tick += 1


def _prompt_sequence_block(bi, t, sinks_ref, gl_ref, x_ref, cond_ref, w_in_ref, w_out_ref, gnw_ref, lnw_ref, lnb_ref,
                           cos_ref, sin_ref, dmat_ref, qdec_ref, kdec_ref,
                           y_ref, kwin_ref, vwin_ref, state_ref, kprev_ref, vprev_ref, alpha):
    n_sub = PROMPT_BLOCK // WINDOW
    x = x_ref[bi]
    shift = cond_ref[bi, :, 0:D_MODEL]
    scale = cond_ref[bi, :, D_MODEL:2 * D_MODEL]
    gate = cond_ref[bi, :, 2 * D_MODEL:3 * D_MODEL]
    h = (x * (1.0 + scale) + shift).astype(BF16)

    def proj(off, width):
        return _dot(h, w_in_ref[:, off:off + width])

    pending = [(name, off + c)
               for name, off, width in (("ga", OFF_GA, D_ATT), ("qr", OFF_QR, D_RET), ("kr", OFF_KR, D_RET),
                                        ("vr", OFF_VR, D_RET), ("gr", OFF_GR, D_RET))
               for c in range(0, width, DENSE_PIECE)]
    n_pending = len(pending)
    pieces = {}

    def emit_dense(n):
        for _ in range(min(n, len(pending))):
            name, off = pending.pop(0)
            pieces.setdefault(name, []).append(proj(off, DENSE_PIECE))

    def section(name):
        return jnp.concatenate(pieces[name], axis=-1)

    zkv = proj(OFF_KA, 2 * KV_W)
    zk = zkv[:, 0:KV_W]
    zv = zkv[:, KV_W:2 * KV_W]

    kwin_ref[bi] = zk[PROMPT_BLOCK - WINDOW:]
    vwin_ref[bi] = zv[PROMPT_BLOCK - WINDOW:]

    zk_b = zk.astype(BF16)
    zv_b = zv.astype(BF16)
    yield
    zq = (jnp.concatenate([proj(OFF_QA + c, DENSE_PIECE) for c in range(0, D_ATT, DENSE_PIECE)], axis=-1)
          * (HEAD_DIM_A ** -0.5)).astype(BF16)
    yield

    kj = lax.broadcasted_iota(jnp.int32, (2 * WINDOW, WINDOW), 0)
    qi = lax.broadcasted_iota(jnp.int32, (2 * WINDOW, WINDOW), 1)
    band = (kj >= qi) & (kj <= qi + WINDOW)

    units = []
    for s in range(n_sub):
        r0 = s * WINDOW
        k_prev = kprev_ref[bi] if s == 0 else zk_b[r0 - WINDOW:r0]
        v_prev = vprev_ref[bi] if s == 0 else zv_b[r0 - WINDOW:r0]
        kk = jnp.concatenate([k_prev, zk_b[r0:r0 + WINDOW]], axis=0)
        vv = jnp.concatenate([v_prev, zv_b[r0:r0 + WINDOW]], axis=0)
        start = t * PROMPT_BLOCK + r0
        valid = band & (kj >= WINDOW - start)
        for hk in range(N_KV_HEADS_A):
            kh = kk[:, hk * HEAD_DIM_A:(hk + 1) * HEAD_DIM_A]
            vh = vv[:, hk * HEAD_DIM_A:(hk + 1) * HEAD_DIM_A]
            qs = jnp.concatenate(
                [zq[r0:r0 + WINDOW, (hk * GROUP_A + g) * HEAD_DIM_A:(hk * GROUP_A + g + 1) * HEAD_DIM_A]
                 for g in range(GROUP_A)], axis=0)
            units.append((hk, valid, _dot_nt(kh, qs), vh))
    kprev_ref[bi] = zk_b[PROMPT_BLOCK - WINDOW:]
    vprev_ref[bi] = zv_b[PROMPT_BLOCK - WINDOW:]
    yield

    fill = -(-n_pending // len(units))
    softmaxed = []
    for hk, valid, sc, vh in units:
        ps, denoms = [], []
        for g in range(GROUP_A):
            sg = jnp.where(valid, sc[:, g * WINDOW:(g + 1) * WINDOW], MASKED)
            p, denom = _sink_softmax(sg, sinks_ref[hk * GROUP_A + g], axis=0)
            ps.append(p)
            denoms.append(denom)
        softmaxed.append((jnp.concatenate(ps, axis=-1), jnp.concatenate(denoms, axis=-1), vh))
        emit_dense(fill)
        yield
    emit_dense(len(pending))

    tiles = []
    for p, denom, vh in softmaxed:
        o_t = _dot_tn(vh, p) / denom
        for pr in range(GROUP_A // 2):
            pair = jnp.concatenate([o_t[:, (2 * pr + e) * WINDOW:(2 * pr + e + 1) * WINDOW] for e in range(2)], axis=0)
            tiles.append(pair.T)
    yield
    tiles_per_sub = N_HEADS_A // 2
    o_a = jnp.concatenate(
        [jnp.concatenate(tiles[s * tiles_per_sub:(s + 1) * tiles_per_sub], axis=-1) for s in range(n_sub)], axis=0)
    mixed_a = (o_a * _silu(section("ga"))).astype(BF16)
    yield

    zqr, zkr, zvr = section("qr"), section("kr"), section("vr")
    cos_full = cos_ref[...]
    sin_signed = sin_ref[...]
    chunk_rows = [slice(s * CHUNK_R, (s + 1) * CHUNK_R) for s in range(n_sub)]
    q_b, v_b, raw, kv, y_a = [], [], [], [], []
    for hh in range(N_HEADS_R):
        cs = slice(hh * HEAD_DIM_R, (hh + 1) * HEAD_DIM_R)
        q_h = _rotate(zqr[:, cs], cos_full, sin_signed).astype(BF16)
        k_h = _rotate(zkr[:, cs], cos_full, sin_signed) * (HEAD_DIM_R ** -0.5)
        v_h = zvr[:, cs].astype(BF16)
        k_hb = k_h.astype(BF16)
        q_b.append(q_h)
        v_b.append(v_h)
        raw.append([_dot_nt(q_h[rs], k_hb[rs]) for rs in chunk_rows])
        kv.append([_dot_tn((k_h[rs] * kdec_ref[hh]).astype(BF16), v_h[rs]) for rs in chunk_rows])
        cols = slice(hh * DENSE_PIECE, (hh + 1) * DENSE_PIECE)
        y_a.append(_dot(mixed_a, w_out_ref[0:D_ATT, cols]))
        yield
    head_cols = []
    for hh in range(N_HEADS_R):
        state = state_ref[bi, hh]
        chunks = []
        for s, rs in enumerate(chunk_rows):
            sc = (raw[hh][s] * dmat_ref[hh]).astype(BF16)
            chunks.append(_dot(sc, v_b[hh][rs]) + _dot(q_b[hh][rs], state.astype(BF16)) * qdec_ref[hh])
            state = gl_ref[hh] * state + kv[hh][s]
        state_ref[bi, hh] = state
        head_cols.append(_group_norm(jnp.concatenate(chunks, axis=0)))
        yield
    o_r = jnp.concatenate(head_cols, axis=-1) * gnw_ref[...]
    mixed_r = (o_r * _silu(section("gr"))).astype(BF16)
    yield

    y = jnp.concatenate(y_a, axis=-1) + _dot(mixed_r, w_out_ref[D_ATT:, :])
    y_ref[bi] = _deepnorm_out(x, gate, y, alpha, lnw_ref[...], lnb_ref[...])


def _prompt_call(x, cond, sinks, w_in_b, w_out_b, gn_w, ln_w, ln_b, consts, alpha):
    batch, seq, _ = x.shape
    n_t = seq // PROMPT_BLOCK
    nb = PROMPT_BATCH_BLOCK
    smem = pl.BlockSpec(memory_space=pltpu.SMEM)
    whole = lambda shape: pl.BlockSpec(shape, lambda b, t: (0,) * len(shape))
    return pl.pallas_call(
        functools.partial(_prompt_kernel, alpha=alpha),
        out_shape=(
            jax.ShapeDtypeStruct((batch, seq, D_MODEL), F32),
            jax.ShapeDtypeStruct((batch, WINDOW, KV_W), F32),
            jax.ShapeDtypeStruct((batch, WINDOW, KV_W), F32),
            jax.ShapeDtypeStruct((batch, N_HEADS_R, HEAD_DIM_R, HEAD_DIM_R), F32),
        ),
        grid=(batch // nb, n_t),
        in_specs=[
            smem, smem,
            pl.BlockSpec((nb, PROMPT_BLOCK, D_MODEL), lambda b, t: (b, t, 0)),
            pl.BlockSpec((nb, 1, 3 * D_MODEL), lambda b, t: (b, 0, 0)),
            whole((D_MODEL, D_IN)),
            whole((D_MODEL, D_MODEL)),
            whole((1, D_RET)),
            whole((1, D_MODEL)),
            whole((1, D_MODEL)),
            pl.BlockSpec((PROMPT_BLOCK, HEAD_DIM_R), lambda b, t: (t, 0)),
            pl.BlockSpec((PROMPT_BLOCK, HEAD_DIM_R), lambda b, t: (t, 0)),
            whole((N_HEADS_R, CHUNK_R, CHUNK_R)),
            whole((N_HEADS_R, CHUNK_R, HEAD_DIM_R)),
            whole((N_HEADS_R, CHUNK_R, HEAD_DIM_R)),
        ],
        out_specs=(
            pl.BlockSpec((nb, PROMPT_BLOCK, D_MODEL), lambda b, t: (b, t, 0)),
            pl.BlockSpec((nb, WINDOW, KV_W), lambda b, t: (b, 0, 0)),
            pl.BlockSpec((nb, WINDOW, KV_W), lambda b, t: (b, 0, 0)),
            pl.BlockSpec((nb, N_HEADS_R, HEAD_DIM_R, HEAD_DIM_R), lambda b, t: (b, 0, 0, 0)),
        ),
        scratch_shapes=[pltpu.VMEM((nb, WINDOW, KV_W), BF16), pltpu.VMEM((nb, WINDOW, KV_W), BF16)],
        compiler_params=pltpu.CompilerParams(
            dimension_semantics=("arbitrary", "arbitrary"), vmem_limit_bytes=VMEM_LIMIT_BYTES),
        name="prompt_layer",
    )(sinks, consts["gl_chunk"], x, cond, w_in_b, w_out_b, gn_w, ln_w, ln_b,
      consts["cos_p"], consts["sin_p"], consts["dmat"], consts["qdec"], consts["kdec"])


def _sample_kernel(sinks_ref, gl_ref, x_ref, cond_ref, ck_ref, cv_ref, sin_state_ref, w_in_ref, w_out_ref,
                   gnw_ref, lnw_ref, lnb_ref, cos_ref, sin_ref, mask_ref, dmat_ref, qdec_ref, kdec_ref,
                   y_ref, kwin_ref, vwin_ref, state_ref, *, alpha, dec_seq):
    bb = SAMPLE_BATCH_BLOCK
    rows = bb * dec_seq
    x3 = x_ref[...]
    shift = cond_ref[:, :, 0:D_MODEL]
    scale = cond_ref[:, :, D_MODEL:2 * D_MODEL]
    gate3 = cond_ref[:, :, 2 * D_MODEL:3 * D_MODEL]
    h = (x3 * (1.0 + scale) + shift).reshape(rows, D_MODEL).astype(BF16)
    x = x3.reshape(rows, D_MODEL)
    gate = jnp.broadcast_to(gate3, (bb, dec_seq, D_MODEL)).reshape(rows, D_MODEL)

    def proj(off, width):
        return _dot(h, w_in_ref[:, off:off + width])

    grp = SAMPLE_ATT_GROUP
    grp_rows = grp * dec_seq
    zkv = proj(OFF_KA, 2 * KV_W)
    zk = zkv[:, 0:KV_W]
    zv = zkv[:, KV_W:2 * KV_W]
    ck = ck_ref[...]
    cv = cv_ref[...]
    kwin_ref[:, 0:WINDOW - dec_seq, :] = ck[:, dec_seq:, :]
    vwin_ref[:, 0:WINDOW - dec_seq, :] = cv[:, dec_seq:, :]
    kwin_ref[:, WINDOW - dec_seq:, :] = zk.reshape(bb, dec_seq, KV_W)
    vwin_ref[:, WINDOW - dec_seq:, :] = zv.reshape(bb, dec_seq, KV_W)
    ck_b = ck.astype(BF16)
    cv_b = cv.astype(BF16)
    zk_b = zk.astype(BF16)
    zv_b = zv.astype(BF16)
    zq = (proj(OFF_QA, D_ATT) * (HEAD_DIM_A ** -0.5)).astype(BF16)
    valid = mask_ref[...] > 0.5
    lane = lax.broadcasted_iota(jnp.int32, (1, GROUP_A * grp_rows), 1)
    sink_rows = []
    for hk in range(N_KV_HEADS_A):
        sink = jnp.full((1, GROUP_A * grp_rows), sinks_ref[hk * GROUP_A], F32)
        for g in range(1, GROUP_A):
            sink = jnp.where(lane >= g * grp_rows, sinks_ref[hk * GROUP_A + g], sink)
        sink_rows.append(sink)
    scores = []
    for gi in range(bb // grp):
        rws = slice(gi * grp_rows, (gi + 1) * grp_rows)
        keys = jnp.concatenate([ck_b[gi * grp:(gi + 1) * grp].reshape(grp * WINDOW, KV_W), zk_b[rws]], axis=0)
        vals = jnp.concatenate([cv_b[gi * grp:(gi + 1) * grp].reshape(grp * WINDOW, KV_W), zv_b[rws]], axis=0)
        for hk in range(N_KV_HEADS_A):
            kh = keys[:, hk * HEAD_DIM_A:(hk + 1) * HEAD_DIM_A]
            vh = vals[:, hk * HEAD_DIM_A:(hk + 1) * HEAD_DIM_A]
            qs = jnp.concatenate(
                [zq[rws, (hk * GROUP_A + g) * HEAD_DIM_A:(hk * GROUP_A + g + 1) * HEAD_DIM_A]
                 for g in range(GROUP_A)], axis=0)
            scores.append((hk, _dot_nt(kh, qs), vh))
    softmaxed = []
    for hk, sc, vh in scores:
        p, denom = _sink_softmax(jnp.where(valid, sc, MASKED), sink_rows[hk], axis=0)
        softmaxed.append((p, denom, vh))
    group_rows = []
    for gi in range(bb // grp):
        pair = jnp.concatenate(
            [_dot_tn(vh, p) / denom for p, denom, vh in softmaxed[gi * N_KV_HEADS_A:(gi + 1) * N_KV_HEADS_A]],
            axis=0).T
        group_rows.append(jnp.concatenate(
            [pair[g * grp_rows:(g + 1) * grp_rows, hk * HEAD_DIM_A:(hk + 1) * HEAD_DIM_A]
             for hk in range(N_KV_HEADS_A) for g in range(GROUP_A)], axis=-1))
    o_a = jnp.concatenate(group_rows, axis=0)
    mixed_a = (o_a * _silu(proj(OFF_GA, D_ATT))).astype(BF16)

    zqr = proj(OFF_QR, D_RET)
    zkr = proj(OFF_KR, D_RET)
    zvr = proj(OFF_VR, D_RET)
    cos_full = cos_ref[...]
    sin_signed = sin_ref[...]
    head_cols = []
    for hh in range(N_HEADS_R):
        cs = slice(hh * HEAD_DIM_R, (hh + 1) * HEAD_DIM_R)
        q_rot = _rotate(zqr[:, cs], cos_full, sin_signed)
        k_rot = _rotate(zkr[:, cs], cos_full, sin_signed) * (HEAD_DIM_R ** -0.5)
        v_h = zvr[:, cs]
        k_dec = k_rot * kdec_ref[hh]
        sc = (_dot_nt(q_rot.astype(BF16), k_rot.astype(BF16)) * dmat_ref[hh]).astype(BF16)
        intra = _dot(sc, v_h.astype(BF16))
        inter = []
        for b in range(bb):
            rs = slice(b * dec_seq, (b + 1) * dec_seq)
            state = sin_state_ref[b, hh]
            inter.append(_dot(q_rot[rs].astype(BF16), state.astype(BF16)))
            state_ref[b, hh] = gl_ref[hh] * state + _dot_tn(k_dec[rs].astype(BF16), v_h[rs].astype(BF16))
        o = intra + jnp.concatenate(inter, axis=0) * qdec_ref[hh]
        head_cols.append(_group_norm(o))
    o_r = jnp.concatenate(head_cols, axis=-1) * gnw_ref[...]
    mixed_r = (o_r * _silu(proj(OFF_GR, D_RET))).astype(BF16)

    y = _dot(mixed_a, w_out_ref[0:D_ATT, :]) + _dot(mixed_r, w_out_ref[D_ATT:, :])
    out = _deepnorm_out(x, gate, y, alpha, lnw_ref[...], lnb_ref[...])
    y_ref[...] = out.reshape(bb, dec_seq, D_MODEL)


def _sample_call(x, cond, cache_k, cache_v, state, sinks, w_in_b, w_out_b, gn_w, ln_w, ln_b, consts, alpha):
    batch, dec_seq, _ = x.shape
    bb = SAMPLE_BATCH_BLOCK
    rows = bb * dec_seq
    grp_rows = SAMPLE_ATT_GROUP * dec_seq
    smem = pl.BlockSpec(memory_space=pltpu.SMEM)
    whole = lambda shape: pl.BlockSpec(shape, lambda i: (0,) * len(shape))
    return pl.pallas_call(
        functools.partial(_sample_kernel, alpha=alpha, dec_seq=dec_seq),
        out_shape=(
            jax.ShapeDtypeStruct((batch, dec_seq, D_MODEL), F32),
            jax.ShapeDtypeStruct((batch, WINDOW, KV_W), F32),
            jax.ShapeDtypeStruct((batch, WINDOW, KV_W), F32),
            jax.ShapeDtypeStruct((batch, N_HEADS_R, HEAD_DIM_R, HEAD_DIM_R), F32),
        ),
        grid=(batch // bb,),
        in_specs=[
            smem, smem,
            pl.BlockSpec((bb, dec_seq, D_MODEL), lambda i: (i, 0, 0)),
            pl.BlockSpec((bb, 1, 3 * D_MODEL), lambda i: (i, 0, 0)),
            pl.BlockSpec((bb, WINDOW, KV_W), lambda i: (i, 0, 0)),
            pl.BlockSpec((bb, WINDOW, KV_W), lambda i: (i, 0, 0)),
            pl.BlockSpec((bb, N_HEADS_R, HEAD_DIM_R, HEAD_DIM_R), lambda i: (i, 0, 0, 0)),
            whole((D_MODEL, D_IN)),
            whole((D_MODEL, D_MODEL)),
            whole((1, D_RET)),
            whole((1, D_MODEL)),
            whole((1, D_MODEL)),
            whole((rows, HEAD_DIM_R)),
            whole((rows, HEAD_DIM_R)),
            whole((SAMPLE_ATT_GROUP * WINDOW + grp_rows, GROUP_A * grp_rows)),
            whole((N_HEADS_R, rows, rows)),
            whole((N_HEADS_R, rows, HEAD_DIM_R)),
            whole((N_HEADS_R, rows, HEAD_DIM_R)),
        ],
        out_specs=(
            pl.BlockSpec((bb, dec_seq, D_MODEL), lambda i: (i, 0, 0)),
            pl.BlockSpec((bb, WINDOW, KV_W), lambda i: (i, 0, 0)),
            pl.BlockSpec((bb, WINDOW, KV_W), lambda i: (i, 0, 0)),
            pl.BlockSpec((bb, N_HEADS_R, HEAD_DIM_R, HEAD_DIM_R), lambda i: (i, 0, 0, 0)),
        ),
        compiler_params=pltpu.CompilerParams(
            dimension_semantics=("arbitrary",), vmem_limit_bytes=VMEM_LIMIT_BYTES),
        name="sample_layer",
    )(sinks, consts["gl_dec"], x, cond, cache_k, cache_v, state, w_in_b, w_out_b, gn_w, ln_w, ln_b,
      consts["cos_s"], consts["sin_s"], consts["mask_s"], consts["dmat_s"], consts["qdec_s"], consts["kdec_s"])


def _rope_tables(pos):
    half = HEAD_DIM_R // 2
    inv = ROPE_BASE ** (-np.arange(half, dtype=np.float64) / half)
    ang = pos[:, None].astype(np.float64) * inv[None, :]
    cos, sin = np.cos(ang), np.sin(ang)
    return (np.concatenate([cos, cos], axis=1).astype(np.float32),
            np.concatenate([-sin, sin], axis=1).astype(np.float32))


def _decay_tables(length):
    log_gamma = np.log(1.0 - 2.0 ** (-5.0 - np.arange(N_HEADS_R, dtype=np.float64)))
    idx = np.arange(length, dtype=np.float64)
    diff = idx[:, None] - idx[None, :]
    dmat = np.where(diff >= 0, np.exp(log_gamma[:, None, None] * np.maximum(diff, 0.0)), 0.0)
    qdec = np.exp(log_gamma[:, None] * (idx[None, :] + 1.0))
    kdec = np.exp(log_gamma[:, None] * (length - 1.0 - idx[None, :]))
    lanes = lambda a: np.broadcast_to(a[:, :, None], (N_HEADS_R, length, HEAD_DIM_R)).astype(np.float32)
    return dmat.astype(np.float32), lanes(qdec), lanes(kdec), np.exp(log_gamma * length).astype(np.float32)


def _sample_mask(dec_seq):
    grp = SAMPLE_ATT_GROUP
    grp_rows = grp * dec_seq
    q = np.arange(GROUP_A * grp_rows)
    q_seq, q_tok = (q % grp_rows) // dec_seq, q % dec_seq
    c = np.arange(grp * WINDOW)
    cached = (c[:, None] // WINDOW == q_seq[None, :]) & (c[:, None] % WINDOW >= q_tok[None, :])
    n = np.arange(grp_rows)
    fresh = (n[:, None] // dec_seq == q_seq[None, :]) & (n[:, None] % dec_seq <= q_tok[None, :])
    return np.concatenate([cached, fresh], axis=0).astype(np.float32)


def _tables(seq, dec_seq):
    cos_p, sin_p = _rope_tables(np.arange(seq))
    cos_s, sin_s = _rope_tables(PAST_LEN + np.arange(dec_seq))
    dmat, qdec, kdec, gl_chunk = _decay_tables(CHUNK_R)
    dmat_s, qdec_s, kdec_s, gl_dec = _decay_tables(dec_seq)
    bb = SAMPLE_BATCH_BLOCK
    tile = lambda a: np.tile(a, (bb, 1))
    dmat_bd = np.stack([np.kron(np.eye(bb, dtype=np.float32), dmat_s[hh]) for hh in range(N_HEADS_R)])
    tile_h = lambda a: np.stack([tile(a[hh]) for hh in range(N_HEADS_R)])
    return dict(cos_p=cos_p, sin_p=sin_p, cos_s=tile(cos_s), sin_s=tile(sin_s), mask_s=_sample_mask(dec_seq),
                dmat=dmat, qdec=qdec, kdec=kdec, gl_chunk=gl_chunk,
                dmat_s=dmat_bd, qdec_s=tile_h(qdec_s), kdec_s=tile_h(kdec_s), gl_dec=gl_dec)


def kernel(x_prompt, x_sample, c_prompt, c_sample, cache_k_win, cache_v_win, state_ret,
           w_ada, b_ada, w_in, attn_sinks, ret_gn_w, w_out, ln_w, ln_b):
    depth = w_in.shape[0]
    batch, seq, _ = x_prompt.shape
    dec_batch, dec_seq, _ = x_sample.shape
    assert seq % PROMPT_BLOCK == 0 and batch % PROMPT_BATCH_BLOCK == 0
    assert dec_batch % SAMPLE_BATCH_BLOCK == 0 and dec_seq == 8
    alpha = float((2 * depth) ** 0.25)
    consts = {k: jnp.asarray(v) for k, v in _tables(seq, dec_seq).items()}
    c_all = jnp.concatenate([c_prompt, c_sample], axis=0)
    x_p, x_s = x_prompt, x_sample
    kp, vp, sp, ks, vs, ss = [], [], [], [], [], []
    for l in range(depth):
        cond = _cond_call(c_all, w_ada[l], b_ada[l])
        w_in_b = w_in[l].astype(BF16)
        w_out_b = w_out[l].astype(BF16)
        gn_w = ret_gn_w[l].reshape(1, D_RET)
        lw = ln_w[l].reshape(1, D_MODEL)
        lb = ln_b[l].reshape(1, D_MODEL)
        x_p, k_p, v_p, s_p = _prompt_call(
            x_p, cond[:batch].reshape(batch, 1, 3 * D_MODEL), attn_sinks[l], w_in_b, w_out_b, gn_w, lw, lb,
            consts, alpha)
        x_s, k_s, v_s, s_s = _sample_call(
            x_s, cond[batch:].reshape(dec_batch, 1, 3 * D_MODEL),
            cache_k_win[l].reshape(dec_batch, WINDOW, KV_W), cache_v_win[l].reshape(dec_batch, WINDOW, KV_W),
            state_ret[l], attn_sinks[l], w_in_b, w_out_b, gn_w, lw, lb, consts, alpha)
        kp.append(k_p.reshape(batch, WINDOW, N_KV_HEADS_A, HEAD_DIM_A))
        vp.append(v_p.reshape(batch, WINDOW, N_KV_HEADS_A, HEAD_DIM_A))
        sp.append(s_p)
        ks.append(k_s.reshape(dec_batch, WINDOW, N_KV_HEADS_A, HEAD_DIM_A))
        vs.append(v_s.reshape(dec_batch, WINDOW, N_KV_HEADS_A, HEAD_DIM_A))
        ss.append(s_s)
    return (x_p, x_s, jnp.stack(kp), jnp.stack(vp), jnp.stack(sp), jnp.stack(ks), jnp.stack(vs), jnp.stack(ss))
```

```python
import functools

import jax
import jax.numpy as jnp
import numpy as np
from jax import lax
from jax.experimental import pallas as pl
from jax.experimental.pallas import tpu as pltpu

D_MODEL = 1024
D_ATT = 512
D_RET = 512
HEAD_DIM_A = 64
N_HEADS_A = 8
N_KV_HEADS_A = 2
GROUP_A = 4
KV_W = 128
WINDOW = 128
N_HEADS_R = 4
HEAD_DIM_R = 128
CHUNK_R = 128
ROPE_BASE = 10000.0
LN_EPS = 1e-5
GN_EPS = 1e-5
PAST_LEN = 16384
MASKED = -1e30

OFF_QA = 0
OFF_KA = OFF_QA + D_ATT
OFF_VA = OFF_KA + KV_W
OFF_GA = OFF_VA + KV_W
OFF_QR = OFF_GA + D_ATT
OFF_KR = OFF_QR + D_RET
OFF_VR = OFF_KR + D_RET
OFF_GR = OFF_VR + D_RET
D_IN = OFF_GR + D_RET

PROMPT_BLOCK = 256
PROMPT_BATCH_BLOCK = 2
PROMPT_STAGE_SKEW = 8
PROMPT_FILL_AFTER_STAGE = (2, 1, 1, 1, 1, 1, 0, 1, 1, 1, 1, 0, 0, 0, 0, 0, 0)
DENSE_PIECE = 256
SAMPLE_BATCH_BLOCK = 16
SAMPLE_ATT_GROUP = 4
COND_COL_BLOCK = 512
VMEM_LIMIT_BYTES = 48 * 1024 * 1024

F32 = jnp.float32
BF16 = jnp.bfloat16


def _silu(x):
    return x * jax.nn.sigmoid(x)


def _dot(a, b):
    return jnp.dot(a, b, preferred_element_type=F32)


def _dot_nt(a, b):
    return lax.dot_general(a, b, (((1,), (1,)), ((), ())), preferred_element_type=F32)


def _dot_tn(a, b):
    return lax.dot_general(a, b, (((0,), (0,)), ((), ())), preferred_element_type=F32)


def _rotate(x, cos_full, sin_signed):
    return x * cos_full + pltpu.roll(x, HEAD_DIM_R // 2, 1) * sin_signed


def _group_norm(o):
    mu = jnp.mean(o, axis=-1, keepdims=True)
    d = o - mu
    var = jnp.mean(d * d, axis=-1, keepdims=True)
    return d * lax.rsqrt(var + GN_EPS)


def _deepnorm_out(x, gate, y, alpha, ln_w, ln_b):
    r = alpha * x + gate * y
    mu = jnp.mean(r, axis=-1, keepdims=True)
    d = r - mu
    var = jnp.mean(d * d, axis=-1, keepdims=True)
    return d * lax.rsqrt(var + LN_EPS) * ln_w + ln_b


def _sink_softmax(s, sink, axis=-1):
    m = jnp.maximum(jnp.max(s, axis=axis, keepdims=True), sink)
    p = jnp.exp(s - m)
    denom = jnp.sum(p, axis=axis, keepdims=True) + jnp.exp(sink - m)
    return p.astype(BF16), denom


def _cond_kernel(c_ref, w_ref, b_ref, o_ref):
    a = _silu(c_ref[...]).astype(BF16)
    o_ref[...] = _dot(a, w_ref[...].astype(BF16)) + b_ref[...]


def _cond_call(c_all, w_ada, b_ada):
    rows = c_all.shape[0]
    n_out = w_ada.shape[1]
    return pl.pallas_call(
        _cond_kernel,
        out_shape=jax.ShapeDtypeStruct((rows, n_out), F32),
        grid=(n_out // COND_COL_BLOCK,),
        in_specs=[
            pl.BlockSpec((rows, D_MODEL), lambda j: (0, 0)),
            pl.BlockSpec((D_MODEL, COND_COL_BLOCK), lambda j: (0, j)),
            pl.BlockSpec((1, COND_COL_BLOCK), lambda j: (0, j)),
        ],
        out_specs=pl.BlockSpec((rows, COND_COL_BLOCK), lambda j: (0, j)),
        compiler_params=pltpu.CompilerParams(dimension_semantics=("arbitrary",)),
        name="adaln_cond",
    )(c_all, w_ada, b_ada.reshape(1, n_out))


class _PromptRefs:
    def __init__(self, **refs):
        self.__dict__.update(refs)


def _prompt_front(r, bi, slot):
    x = r.x_ref[bi]
    shift = r.cond_ref[bi, :, 0:D_MODEL]
    scale = r.cond_ref[bi, :, D_MODEL:2 * D_MODEL]
    h = (x * (1.0 + scale) + shift).astype(BF16)
    yield
    for off in range(0, D_IN, DENSE_PIECE):
        r.z_refs[slot][bi, :, off:off + DENSE_PIECE] = _dot(h, r.w_in_ref[:, off:off + DENSE_PIECE])
        yield


def _prompt_back(r, bi, slot, block_idx):
    n_sub = PROMPT_BLOCK // WINDOW

    def z(off, width):
        return r.z_refs[slot][bi, :, off:off + width]

    zk = z(OFF_KA, KV_W)
    zv = z(OFF_VA, KV_W)
    r.kwin_ref[bi] = zk[PROMPT_BLOCK - WINDOW:]
    r.vwin_ref[bi] = zv[PROMPT_BLOCK - WINDOW:]
    zk_b = zk.astype(BF16)
    zv_b = zv.astype(BF16)
    zq = (z(OFF_QA, D_ATT) * (HEAD_DIM_A ** -0.5)).astype(BF16)

    kj = lax.broadcasted_iota(jnp.int32, (2 * WINDOW, WINDOW), 0)
    qi = lax.broadcasted_iota(jnp.int32, (2 * WINDOW, WINDOW), 1)
    band = (kj >= qi) & (kj <= qi + WINDOW)

    units = []
    for s in range(n_sub):
        r0 = s * WINDOW
        k_prev = r.kprev_ref[bi] if s == 0 else zk_b[r0 - WINDOW:r0]
        v_prev = r.vprev_ref[bi] if s == 0 else zv_b[r0 - WINDOW:r0]
        kk = jnp.concatenate([k_prev, zk_b[r0:r0 + WINDOW]], axis=0)
        vv = jnp.concatenate([v_prev, zv_b[r0:r0 + WINDOW]], axis=0)
        start = block_idx * PROMPT_BLOCK + r0
        valid = band & (kj >= WINDOW - start)
        for hk in range(N_KV_HEADS_A):
            kh = kk[:, hk * HEAD_DIM_A:(hk + 1) * HEAD_DIM_A]
            vh = vv[:, hk * HEAD_DIM_A:(hk + 1) * HEAD_DIM_A]
            qs = jnp.concatenate(
                [zq[r0:r0 + WINDOW, (hk * GROUP_A + g) * HEAD_DIM_A:(hk * GROUP_A + g + 1) * HEAD_DIM_A]
                 for g in range(GROUP_A)], axis=0)
            units.append((hk, valid, _dot_nt(kh, qs), vh))
    r.kprev_ref[bi] = zk_b[PROMPT_BLOCK - WINDOW:]
    r.vprev_ref[bi] = zv_b[PROMPT_BLOCK - WINDOW:]
    yield

    softmaxed = []
    for hk, valid, sc, vh in units:
        ps, denoms = [], []
        for g in range(GROUP_A):
            sg = jnp.where(valid, sc[:, g * WINDOW:(g + 1) * WINDOW], MASKED)
            p, denom = _sink_softmax(sg, r.sinks_ref[hk * GROUP_A + g], axis=0)
            ps.append(p)
            denoms.append(denom)
        softmaxed.append((jnp.concatenate(ps, axis=-1), jnp.concatenate(denoms, axis=-1), vh))
        yield

    tiles = []
    for p, denom, vh in softmaxed:
        o_t = _dot_tn(vh, p) / denom
        for pr in range(GROUP_A // 2):
            pair = jnp.concatenate([o_t[:, (2 * pr + e) * WINDOW:(2 * pr + e + 1) * WINDOW] for e in range(2)], axis=0)
            tiles.append(pair.T)
    yield
    tiles_per_sub = N_HEADS_A // 2
    o_a = jnp.concatenate(
        [jnp.concatenate(tiles[s * tiles_per_sub:(s + 1) * tiles_per_sub], axis=-1) for s in range(n_sub)], axis=0)
    mixed_a = (o_a * _silu(z(OFF_GA, D_ATT))).astype(BF16)
    yield

    cos_full = r.cos_ref[...]
    sin_signed = r.sin_ref[...]
    chunk_rows = [slice(s * CHUNK_R, (s + 1) * CHUNK_R) for s in range(n_sub)]
    q_b, v_b, raw, kv, y_a = [], [], [], [], []
    for hh in range(N_HEADS_R):
        q_h = _rotate(z(OFF_QR + hh * HEAD_DIM_R, HEAD_DIM_R), cos_full, sin_signed).astype(BF16)
        k_h = _rotate(z(OFF_KR + hh * HEAD_DIM_R, HEAD_DIM_R), cos_full, sin_signed) * (HEAD_DIM_R ** -0.5)
        v_h = z(OFF_VR + hh * HEAD_DIM_R, HEAD_DIM_R).astype(BF16)
        k_hb = k_h.astype(BF16)
        q_b.append(q_h)
        v_b.append(v_h)
        raw.append([_dot_nt(q_h[rs], k_hb[rs]) for rs in chunk_rows])
        kv.append([_dot_tn((k_h[rs] * r.kdec_ref[hh]).astype(BF16), v_h[rs]) for rs in chunk_rows])
        cols = slice(hh * DENSE_PIECE, (hh + 1) * DENSE_PIECE)
        y_a.append(_dot(mixed_a, r.w_out_ref[0:D_ATT, cols]))
        yield
    head_cols = []
    for hh in range(N_HEADS_R):
        state = r.state_ref[bi, hh]
        chunks = []
        for s, rs in enumerate(chunk_rows):
            sc = (raw[hh][s] * r.dmat_ref[hh]).astype(BF16)
            chunks.append(_dot(sc, v_b[hh][rs]) + _dot(q_b[hh][rs], state.astype(BF16)) * r.qdec_ref[hh])
            state = r.gl_ref[hh] * state + kv[hh][s]
        r.state_ref[bi, hh] = state
        head_cols.append(_group_norm(jnp.concatenate(chunks, axis=0)))
        yield
    o_r = jnp.concatenate(head_cols, axis=-1) * r.gnw_ref[...]
    mixed_r = (o_r * _silu(z(OFF_GR, D_RET))).astype(BF16)
    yield

    y = jnp.concatenate(y_a, axis=-1) + _dot(mixed_r, r.w_out_ref[D_ATT:, :])
    gate = r.cond_ref[bi, :, 2 * D_MODEL:3 * D_MODEL]
    r.y_ref[bi] = _deepnorm_out(r.xprev_ref[bi], gate, y, r.alpha, r.lnw_ref[...], r.lnb_ref[...])


def _trace_pair(back, front, fill_after_stage):
    for n in fill_after_stage:
        next(back, None)
        for _ in range(n):
            next(front, None)
    for _ in back:
        pass
    for _ in front:
        pass


def _trace_staggered(stages, skew):
    live = list(range(len(stages)))
    tick = 0
    while live:
        for i in list(live):
            if tick >= i * skew and next(stages[i], "done") == "done":
                live.remove(i)
        tick += 1


def _prompt_kernel(sinks_ref, gl_ref, x_ref, xprev_ref, cond_ref, w_in_ref, w_out_ref, gnw_ref, lnw_ref, lnb_ref,
                   cos_ref, sin_ref, dmat_ref, qdec_ref, kdec_ref,
                   y_ref, kwin_ref, vwin_ref, state_ref, kprev_ref, vprev_ref, z0_ref, z1_ref, *, alpha, n_blocks):
    r = _PromptRefs(sinks_ref=sinks_ref, gl_ref=gl_ref, x_ref=x_ref, xprev_ref=xprev_ref, cond_ref=cond_ref,
                    w_in_ref=w_in_ref, w_out_ref=w_out_ref, gnw_ref=gnw_ref, lnw_ref=lnw_ref, lnb_ref=lnb_ref,
                    cos_ref=cos_ref, sin_ref=sin_ref, dmat_ref=dmat_ref, qdec_ref=qdec_ref, kdec_ref=kdec_ref,
                    y_ref=y_ref, kwin_ref=kwin_ref, vwin_ref=vwin_ref, state_ref=state_ref,
                    kprev_ref=kprev_ref, vprev_ref=vprev_ref, z_refs=(z0_ref, z1_ref), alpha=alpha)
    t = pl.program_id(1)
    seqs = range(PROMPT_BATCH_BLOCK)

    @pl.when(t == 0)
    def _():
        kprev_ref[...] = jnp.zeros_like(kprev_ref)
        vprev_ref[...] = jnp.zeros_like(vprev_ref)
        state_ref[...] = jnp.zeros_like(state_ref)
        for bi in seqs:
            for _ in _prompt_front(r, bi, 0):
                pass

    for slot in range(2):
        @pl.when((t > 0) & (t < n_blocks) & (lax.rem(t, 2) == slot))
        def _():
            for bi in seqs:
                _trace_pair(_prompt_back(r, bi, 1 - slot, t - 1), _prompt_front(r, bi, slot),
                            PROMPT_FILL_AFTER_STAGE)

    @pl.when(t == n_blocks)
    def _():
        _trace_staggered([_prompt_back(r, bi, (n_blocks - 1) % 2, n_blocks - 1) for bi in seqs], PROMPT_STAGE_SKEW)


def _prompt_call(x, cond, sinks, w_in_b, w_out_b, gn_w, ln_w, ln_b, consts, alpha):
    batch, seq, _ = x.shape
    n_blocks = seq // PROMPT_BLOCK
    nb = PROMPT_BATCH_BLOCK
    smem = pl.BlockSpec(memory_space=pltpu.SMEM)
    whole = lambda shape: pl.BlockSpec(shape, lambda b, t: (0,) * len(shape))
    cur = lambda t: jnp.minimum(t, n_blocks - 1)
    prev = lambda t: jnp.maximum(t - 1, 0)
    return pl.pallas_call(
        functools.partial(_prompt_kernel, alpha=alpha, n_blocks=n_blocks),
        out_shape=(
            jax.ShapeDtypeStruct((batch, seq, D_MODEL), F32),
            jax.ShapeDtypeStruct((batch, WINDOW, KV_W), F32),
            jax.ShapeDtypeStruct((batch, WINDOW, KV_W), F32),
            jax.ShapeDtypeStruct((batch, N_HEADS_R, HEAD_DIM_R, HEAD_DIM_R), F32),
        ),
        grid=(batch // nb, n_blocks + 1),
        in_specs=[
            smem, smem,
            pl.BlockSpec((nb, PROMPT_BLOCK, D_MODEL), lambda b, t: (b, cur(t), 0)),
            pl.BlockSpec((nb, PROMPT_BLOCK, D_MODEL), lambda b, t: (b, prev(t), 0)),
            pl.BlockSpec((nb, 1, 3 * D_MODEL), lambda b, t: (b, 0, 0)),
            whole((D_MODEL, D_IN)),
            whole((D_MODEL, D_MODEL)),
            whole((1, D_RET)),
            whole((1, D_MODEL)),
            whole((1, D_MODEL)),
            pl.BlockSpec((PROMPT_BLOCK, HEAD_DIM_R), lambda b, t: (prev(t), 0)),
            pl.BlockSpec((PROMPT_BLOCK, HEAD_DIM_R), lambda b, t: (prev(t), 0)),
            whole((N_HEADS_R, CHUNK_R, CHUNK_R)),
            whole((N_HEADS_R, CHUNK_R, HEAD_DIM_R)),
            whole((N_HEADS_R, CHUNK_R, HEAD_DIM_R)),
        ],
        out_specs=(
            pl.BlockSpec((nb, PROMPT_BLOCK, D_MODEL), lambda b, t: (b, prev(t), 0)),
            pl.BlockSpec((nb, WINDOW, KV_W), lambda b, t: (b, 0, 0)),
            pl.BlockSpec((nb, WINDOW, KV_W), lambda b, t: (b, 0, 0)),
            pl.BlockSpec((nb, N_HEADS_R, HEAD_DIM_R, HEAD_DIM_R), lambda b, t: (b, 0, 0, 0)),
        ),
        scratch_shapes=[pltpu.VMEM((nb, WINDOW, KV_W), BF16), pltpu.VMEM((nb, WINDOW, KV_W), BF16),
                        pltpu.VMEM((nb, PROMPT_BLOCK, D_IN), F32), pltpu.VMEM((nb, PROMPT_BLOCK, D_IN), F32)],
        compiler_params=pltpu.CompilerParams(
            dimension_semantics=("arbitrary", "arbitrary"), vmem_limit_bytes=VMEM_LIMIT_BYTES),
        name="prompt_layer",
    )(sinks, consts["gl_chunk"], x, x, cond, w_in_b, w_out_b, gn_w, ln_w, ln_b,
      consts["cos_p"], consts["sin_p"], consts["dmat"], consts["qdec"], consts["kdec"])


def _sample_kernel(sinks_ref, gl_ref, x_ref, cond_ref, ck_ref, cv_ref, sin_state_ref, w_in_ref, w_out_ref,
                   gnw_ref, lnw_ref, lnb_ref, cos_ref, sin_ref, mask_ref, dmat_ref, qdec_ref, kdec_ref,
                   y_ref, kwin_ref, vwin_ref, state_ref, *, alpha, dec_seq):
    bb = SAMPLE_BATCH_BLOCK
    rows = bb * dec_seq
    x3 = x_ref[...]
    shift = cond_ref[:, :, 0:D_MODEL]
    scale = cond_ref[:, :, D_MODEL:2 * D_MODEL]
    gate3 = cond_ref[:, :, 2 * D_MODEL:3 * D_MODEL]
    h = (x3 * (1.0 + scale) + shift).reshape(rows, D_MODEL).astype(BF16)
    x = x3.reshape(rows, D_MODEL)
    gate = jnp.broadcast_to(gate3, (bb, dec_seq, D_MODEL)).reshape(rows, D_MODEL)

    def proj(off, width):
        return _dot(h, w_in_ref[:, off:off + width])

    grp = SAMPLE_ATT_GROUP
    grp_rows = grp * dec_seq
    zkv = proj(OFF_KA, 2 * KV_W)
    zk = zkv[:, 0:KV_W]
    zv = zkv[:, KV_W:2 * KV_W]
    ck = ck_ref[...]
    cv = cv_ref[...]
    kwin_ref[:, 0:WINDOW - dec_seq, :] = ck[:, dec_seq:, :]
    vwin_ref[:, 0:WINDOW - dec_seq, :] = cv[:, dec_seq:, :]
    kwin_ref[:, WINDOW - dec_seq:, :] = zk.reshape(bb, dec_seq, KV_W)
    vwin_ref[:, WINDOW - dec_seq:, :] = zv.reshape(bb, dec_seq, KV_W)
    ck_b = ck.astype(BF16)
    cv_b = cv.astype(BF16)
    zk_b = zk.astype(BF16)
    zv_b = zv.astype(BF16)
    zq = (proj(OFF_QA, D_ATT) * (HEAD_DIM_A ** -0.5)).astype(BF16)
    valid = mask_ref[...] > 0.5
    lane = lax.broadcasted_iota(jnp.int32, (1, GROUP_A * grp_rows), 1)
    sink_rows = []
    for hk in range(N_KV_HEADS_A):
        sink = jnp.full((1, GROUP_A * grp_rows), sinks_ref[hk * GROUP_A], F32)
        for g in range(1, GROUP_A):
            sink = jnp.where(lane >= g * grp_rows, sinks_ref[hk * GROUP_A + g], sink)
        sink_rows.append(sink)
    scores = []
    for gi in range(bb // grp):
        rws = slice(gi * grp_rows, (gi + 1) * grp_rows)
        keys = jnp.concatenate([ck_b[gi * grp:(gi + 1) * grp].reshape(grp * WINDOW, KV_W), zk_b[rws]], axis=0)
        vals = jnp.concatenate([cv_b[gi * grp:(gi + 1) * grp].reshape(grp * WINDOW, KV_W), zv_b[rws]], axis=0)
        for hk in range(N_KV_HEADS_A):
            kh = keys[:, hk * HEAD_DIM_A:(hk + 1) * HEAD_DIM_A]
            vh = vals[:, hk * HEAD_DIM_A:(hk + 1) * HEAD_DIM_A]
            qs = jnp.concatenate(
                [zq[rws, (hk * GROUP_A + g) * HEAD_DIM_A:(hk * GROUP_A + g + 1) * HEAD_DIM_A]
                 for g in range(GROUP_A)], axis=0)
            scores.append((hk, _dot_nt(kh, qs), vh))
    softmaxed = []
    for hk, sc, vh in scores:
        p, denom = _sink_softmax(jnp.where(valid, sc, MASKED), sink_rows[hk], axis=0)
        softmaxed.append((p, denom, vh))
    group_rows = []
    for gi in range(bb // grp):
        pair = jnp.concatenate(
            [_dot_tn(vh, p) / denom for p, denom, vh in softmaxed[gi * N_KV_HEADS_A:(gi + 1) * N_KV_HEADS_A]],
            axis=0).T
        group_rows.append(jnp.concatenate(
            [pair[g * grp_rows:(g + 1) * grp_rows, hk * HEAD_DIM_A:(hk + 1) * HEAD_DIM_A]
             for hk in range(N_KV_HEADS_A) for g in range(GROUP_A)], axis=-1))
    o_a = jnp.concatenate(group_rows, axis=0)
    mixed_a = (o_a * _silu(proj(OFF_GA, D_ATT))).astype(BF16)

    zqr = proj(OFF_QR, D_RET)
    zkr = proj(OFF_KR, D_RET)
    zvr = proj(OFF_VR, D_RET)
    cos_full = cos_ref[...]
    sin_signed = sin_ref[...]
    head_cols = []
    for hh in range(N_HEADS_R):
        cs = slice(hh * HEAD_DIM_R, (hh + 1) * HEAD_DIM_R)
        q_rot = _rotate(zqr[:, cs], cos_full, sin_signed)
        k_rot = _rotate(zkr[:, cs], cos_full, sin_signed) * (HEAD_DIM_R ** -0.5)
        v_h = zvr[:, cs]
        k_dec = k_rot * kdec_ref[hh]
        sc = (_dot_nt(q_rot.astype(BF16), k_rot.astype(BF16)) * dmat_ref[hh]).astype(BF16)
        intra = _dot(sc, v_h.astype(BF16))
        inter = []
        for b in range(bb):
            rs = slice(b * dec_seq, (b + 1) * dec_seq)
            state = sin_state_ref[b, hh]
            inter.append(_dot(q_rot[rs].astype(BF16), state.astype(BF16)))
            state_ref[b, hh] = gl_ref[hh] * state + _dot_tn(k_dec[rs].astype(BF16), v_h[rs].astype(BF16))
        o = intra + jnp.concatenate(inter, axis=0) * qdec_ref[hh]
        head_cols.append(_group_norm(o))
    o_r = jnp.concatenate(head_cols, axis=-1) * gnw_ref[...]
    mixed_r = (o_r * _silu(proj(OFF_GR, D_RET))).astype(BF16)

    y = _dot(mixed_a, w_out_ref[0:D_ATT, :]) + _dot(mixed_r, w_out_ref[D_ATT:, :])
    out = _deepnorm_out(x, gate, y, alpha, lnw_ref[...], lnb_ref[...])
    y_ref[...] = out.reshape(bb, dec_seq, D_MODEL)


def _sample_call(x, cond, cache_k, cache_v, state, sinks, w_in_b, w_out_b, gn_w, ln_w, ln_b, consts, alpha):
    batch, dec_seq, _ = x.shape
    bb = SAMPLE_BATCH_BLOCK
    rows = bb * dec_seq
    grp_rows = SAMPLE_ATT_GROUP * dec_seq
    smem = pl.BlockSpec(memory_space=pltpu.SMEM)
    whole = lambda shape: pl.BlockSpec(shape, lambda i: (0,) * len(shape))
    return pl.pallas_call(
        functools.partial(_sample_kernel, alpha=alpha, dec_seq=dec_seq),
        out_shape=(
            jax.ShapeDtypeStruct((batch, dec_seq, D_MODEL), F32),
            jax.ShapeDtypeStruct((batch, WINDOW, KV_W), F32),
            jax.ShapeDtypeStruct((batch, WINDOW, KV_W), F32),
            jax.ShapeDtypeStruct((batch, N_HEADS_R, HEAD_DIM_R, HEAD_DIM_R), F32),
        ),
        grid=(batch // bb,),
        in_specs=[
            smem, smem,
            pl.BlockSpec((bb, dec_seq, D_MODEL), lambda i: (i, 0, 0)),
            pl.BlockSpec((bb, 1, 3 * D_MODEL), lambda i: (i, 0, 0)),
            pl.BlockSpec((bb, WINDOW, KV_W), lambda i: (i, 0, 0)),
            pl.BlockSpec((bb, WINDOW, KV_W), lambda i: (i, 0, 0)),
            pl.BlockSpec((bb, N_HEADS_R, HEAD_DIM_R, HEAD_DIM_R), lambda i: (i, 0, 0, 0)),
            whole((D_MODEL, D_IN)),
            whole((D_MODEL, D_MODEL)),
            whole((1, D_RET)),
            whole((1, D_MODEL)),
            whole((1, D_MODEL)),
            whole((rows, HEAD_DIM_R)),
            whole((rows, HEAD_DIM_R)),
            whole((SAMPLE_ATT_GROUP * WINDOW + grp_rows, GROUP_A * grp_rows)),
            whole((N_HEADS_R, rows, rows)),
            whole((N_HEADS_R, rows, HEAD_DIM_R)),
            whole((N_HEADS_R, rows, HEAD_DIM_R)),
        ],
        out_specs=(
            pl.BlockSpec((bb, dec_seq, D_MODEL), lambda i: (i, 0, 0)),
            pl.BlockSpec((bb, WINDOW, KV_W), lambda i: (i, 0, 0)),
            pl.BlockSpec((bb, WINDOW, KV_W), lambda i: (i, 0, 0)),
            pl.BlockSpec((bb, N_HEADS_R, HEAD_DIM_R, HEAD_DIM_R), lambda i: (i, 0, 0, 0)),
        ),
        compiler_params=pltpu.CompilerParams(
            dimension_semantics=("arbitrary",), vmem_limit_bytes=VMEM_LIMIT_BYTES),
        name="sample_layer",
    )(sinks, consts["gl_dec"], x, cond, cache_k, cache_v, state, w_in_b, w_out_b, gn_w, ln_w, ln_b,
      consts["cos_s"], consts["sin_s"], consts["mask_s"], consts["dmat_s"], consts["qdec_s"], consts["kdec_s"])


def _rope_tables(pos):
    half = HEAD_DIM_R // 2
    inv = ROPE_BASE ** (-np.arange(half, dtype=np.float64) / half)
    ang = pos[:, None].astype(np.float64) * inv[None, :]
    cos, sin = np.cos(ang), np.sin(ang)
    return (np.concatenate([cos, cos], axis=1).astype(np.float32),
            np.concatenate([-sin, sin], axis=1).astype(np.float32))


def _decay_tables(length):
    log_gamma = np.log(1.0 - 2.0 ** (-5.0 - np.arange(N_HEADS_R, dtype=np.float64)))
    idx = np.arange(length, dtype=np.float64)
    diff = idx[:, None] - idx[None, :]
    dmat = np.where(diff >= 0, np.exp(log_gamma[:, None, None] * np.maximum(diff, 0.0)), 0.0)
    qdec = np.exp(log_gamma[:, None] * (idx[None, :] + 1.0))
    kdec = np.exp(log_gamma[:, None] * (length - 1.0 - idx[None, :]))
    lanes = lambda a: np.broadcast_to(a[:, :, None], (N_HEADS_R, length, HEAD_DIM_R)).astype(np.float32)
    return dmat.astype(np.float32), lanes(qdec), lanes(kdec), np.exp(log_gamma * length).astype(np.float32)


def _sample_mask(dec_seq):
    grp = SAMPLE_ATT_GROUP
    grp_rows = grp * dec_seq
    q = np.arange(GROUP_A * grp_rows)
    q_seq, q_tok = (q % grp_rows) // dec_seq, q % dec_seq
    c = np.arange(grp * WINDOW)
    cached = (c[:, None] // WINDOW == q_seq[None, :]) & (c[:, None] % WINDOW >= q_tok[None, :])
    n = np.arange(grp_rows)
    fresh = (n[:, None] // dec_seq == q_seq[None, :]) & (n[:, None] % dec_seq <= q_tok[None, :])
    return np.concatenate([cached, fresh], axis=0).astype(np.float32)


def _tables(seq, dec_seq):
    cos_p, sin_p = _rope_tables(np.arange(seq))
    cos_s, sin_s = _rope_tables(PAST_LEN + np.arange(dec_seq))
    dmat, qdec, kdec, gl_chunk = _decay_tables(CHUNK_R)
    dmat_s, qdec_s, kdec_s, gl_dec = _decay_tables(dec_seq)
    bb = SAMPLE_BATCH_BLOCK
    tile = lambda a: np.tile(a, (bb, 1))
    dmat_bd = np.stack([np.kron(np.eye(bb, dtype=np.float32), dmat_s[hh]) for hh in range(N_HEADS_R)])
    tile_h = lambda a: np.stack([tile(a[hh]) for hh in range(N_HEADS_R)])
    return dict(cos_p=cos_p, sin_p=sin_p, cos_s=tile(cos_s), sin_s=tile(sin_s), mask_s=_sample_mask(dec_seq),
                dmat=dmat, qdec=qdec, kdec=kdec, gl_chunk=gl_chunk,
                dmat_s=dmat_bd, qdec_s=tile_h(qdec_s), kdec_s=tile_h(kdec_s), gl_dec=gl_dec)


def kernel(x_prompt, x_sample, c_prompt, c_sample, cache_k_win, cache_v_win, state_ret,
           w_ada, b_ada, w_in, attn_sinks, ret_gn_w, w_out, ln_w, ln_b):
    depth = w_in.shape[0]
    batch, seq, _ = x_prompt.shape
    dec_batch, dec_seq, _ = x_sample.shape
    assert seq % PROMPT_BLOCK == 0 and batch % PROMPT_BATCH_BLOCK == 0
    assert dec_batch % SAMPLE_BATCH_BLOCK == 0 and dec_seq == 8
    alpha = float((2 * depth) ** 0.25)
    consts = {k: jnp.asarray(v) for k, v in _tables(seq, dec_seq).items()}
    c_all = jnp.concatenate([c_prompt, c_sample], axis=0)
    x_p, x_s = x_prompt, x_sample
    kp, vp, sp, ks, vs, ss = [], [], [], [], [], []
    for l in range(depth):
        cond = _cond_call(c_all, w_ada[l], b_ada[l])
        w_in_b = w_in[l].astype(BF16)
        w_out_b = w_out[l].astype(BF16)
        gn_w = ret_gn_w[l].reshape(1, D_RET)
        lw = ln_w[l].reshape(1, D_MODEL)
        lb = ln_b[l].reshape(1, D_MODEL)
        x_p, k_p, v_p, s_p = _prompt_call(
            x_p, cond[:batch].reshape(batch, 1, 3 * D_MODEL), attn_sinks[l], w_in_b, w_out_b, gn_w, lw, lb,
            consts, alpha)
        x_s, k_s, v_s, s_s = _sample_call(
            x_s, cond[batch:].reshape(dec_batch, 1, 3 * D_MODEL),
            cache_k_win[l].reshape(dec_batch, WINDOW, KV_W), cache_v_win[l].reshape(dec_batch, WINDOW, KV_W),
            state_ret[l], attn_sinks[l], w_in_b, w_out_b, gn_w, lw, lb, consts, alpha)
        kp.append(k_p.reshape(batch, WINDOW, N_KV_HEADS_A, HEAD_DIM_A))
        vp.append(v_p.reshape(batch, WINDOW, N_KV_HEADS_A, HEAD_DIM_A))
        sp.append(s_p)
        ks.append(k_s.reshape(dec_batch, WINDOW, N_KV_HEADS_A, HEAD_DIM_A))
        vs.append(v_s.reshape(dec_batch, WINDOW, N_KV_HEADS_A, HEAD_DIM_A))
        ss.append(s_s)
    return (x_p, x_s, jnp.stack(kp), jnp.stack(vp), jnp.stack(sp), jnp.stack(ks), jnp.stack(vs), jnp.stack(ss))
```
